```python
import math
import jax, jax.numpy as jnp
from jax import lax
import numpy as np


D_MODEL = 1024
BATCH = 16
SEQ = 4096
DEPTH = 1

N_META = 16
BLK = 128
WINDOW = 128
ROPE_THETA = 10000.0
EPS = 1e-6
NEG_INF = -1e30

SWA_HQ = 8
SWA_HKV = 2
SWA_G = SWA_HQ // SWA_HKV
SWA_DH = 64

DIFF_H = 8
DIFF_DQ = 32
DIFF_DV = 2 * DIFF_DQ

IN_WIDTHS = (SWA_HQ * SWA_DH, SWA_HKV * SWA_DH, SWA_HKV * SWA_DH,
             DIFF_H * 2 * DIFF_DQ, DIFF_H * 2 * DIFF_DQ, DIFF_H * DIFF_DV)
IN_WIDTH = 2304
MIX_WIDTH = SWA_HQ * SWA_DH + DIFF_H * DIFF_DV

N_EXPERTS = 32
TOP_K = 4
D_FF = D_MODEL
SWIGLU_LIMIT = 7.0
SWIGLU_ALPHA = 1.702
MOE_BLK = 128

kernel_name = "hymba_swa_sink_diffattn_gptoss_moe"


def _rmsnorm(x, g, eps=EPS):
    xf = x.astype(jnp.float32)
    y = xf * lax.rsqrt(jnp.mean(xf * xf, axis=-1, keepdims=True) + eps)
    return (y * g.astype(jnp.float32)).astype(x.dtype)


def _rope(x, pos):
    d = x.shape[-1]
    inv = ROPE_THETA ** (-jnp.arange(0, d, 2, dtype=jnp.float32) / d)
    ang = pos.astype(jnp.float32)[:, None] * inv[None, :]
    cos = jnp.cos(ang)[None, :, None, :]
    sin = jnp.sin(ang)[None, :, None, :]
    xf = x.astype(jnp.float32)
    x1, x2 = xf[..., : d // 2], xf[..., d // 2:]
    out = jnp.concatenate([x1 * cos - x2 * sin, x2 * cos + x1 * sin], axis=-1)
    return out.astype(x.dtype)


def _band(t):
    pad = [(0, 0)] * t.ndim
    pad[1] = (1, 1)
    tp = jnp.pad(t, pad)
    return jnp.concatenate([tp[:, :-2], tp[:, 1:-1], tp[:, 2:]], axis=2)


def _swa_sink_attention(q, k, v, sink, pos):
    B, L = q.shape[0], q.shape[1]
    P = BLK - N_META
    nb = (L + P) // BLK
    front = ((0, 0), (P, 0), (0, 0), (0, 0))
    qb = jnp.pad(q, front).reshape(B, nb, BLK, SWA_HKV, SWA_G, SWA_DH)
    kb = jnp.pad(k, front).reshape(B, nb, BLK, SWA_HKV, SWA_DH)
    vb = jnp.pad(v, front).reshape(B, nb, BLK, SWA_HKV, SWA_DH)
    k_band, v_band = _band(kb), _band(vb)
    k_meta, v_meta = k[:, :N_META], v[:, :N_META]
    pb = jnp.concatenate([jnp.full((P,), -1, jnp.int32), pos]).reshape(nb, BLK)
    pbp = jnp.pad(pb, ((1, 1), (0, 0)), constant_values=-1)
    p_band = jnp.concatenate([pbp[:-2], pbp[1:-1], pbp[2:]], axis=1)
    band_ok = (p_band[:, None, :] >= N_META) & (jnp.abs(pb[:, :, None] - p_band[:, None, :]) <= WINDOW)
    scale = SWA_DH ** -0.5
    s_meta = jnp.einsum('bnqhgd,bkhd->bnhgqk', qb, k_meta, preferred_element_type=jnp.float32) * scale
    s_band = jnp.einsum('bnqhgd,bnkhd->bnhgqk', qb, k_band, preferred_element_type=jnp.float32) * scale
    s_band = jnp.where(band_ok[None, :, None, None], s_band, NEG_INF)
    s_sink = jnp.broadcast_to(sink.astype(jnp.float32).reshape(1, 1, SWA_HKV, SWA_G, 1, 1),
                              s_meta.shape[:-1] + (1,))
    probs = jax.nn.softmax(jnp.concatenate([s_meta, s_band, s_sink], axis=-1), axis=-1).astype(v.dtype)
    o = (jnp.einsum('bnhgqk,bkhd->bnqhgd', probs[..., :N_META], v_meta)
         + jnp.einsum('bnhgqk,bnkhd->bnqhgd', probs[..., N_META:N_META + 3 * BLK], v_band))
    return o.reshape(B, nb * BLK, SWA_HQ * SWA_DH)[:, P:]


def _diff_attention(q1, q2, k1, k2, v, lam):
    B, L = q1.shape[0], q1.shape[1]
    nq = -(-L // BLK)
    pad_q = ((0, 0), (0, nq * BLK - L), (0, 0), (0, 0))

    def to_blocks(t):
        return jnp.moveaxis(jnp.pad(t, pad_q).reshape(B, nq, BLK, DIFF_H, DIFF_DQ), 1, 0)

    scale = DIFF_DQ ** -0.5

    def one_block(qs):
        a, b = qs
        p1 = jax.nn.softmax(jnp.einsum('bqhd,bkhd->bhqk', a, k1, preferred_element_type=jnp.float32) * scale, axis=-1)
        p2 = jax.nn.softmax(jnp.einsum('bqhd,bkhd->bhqk', b, k2, preferred_element_type=jnp.float32) * scale, axis=-1)
        return jnp.einsum('bhqk,bkhd->bqhd', (p1 - lam * p2).astype(v.dtype), v)

    o = lax.map(one_block, (to_blocks(q1), to_blocks(q2)))
    return jnp.moveaxis(o, 0, 1).reshape(B, nq * BLK, DIFF_H, DIFF_DV)[:, :L]


def _clamped_swiglu(a):
    glu, lin = a[..., :D_FF], a[..., D_FF:]
    glu = jnp.minimum(glu, SWIGLU_LIMIT)
    lin = jnp.clip(lin, -SWIGLU_LIMIT, SWIGLU_LIMIT)
    return glu * jax.nn.sigmoid(SWIGLU_ALPHA * glu) * (lin + 1.0)


def _moe(h, w_router, b_router, w_mlp1, b_mlp1, w_mlp2, b_mlp2):
    T, D = h.shape
    logits = (h @ w_router + b_router).astype(jnp.float32)
    top_val, top_idx = lax.top_k(logits, TOP_K)
    gates = jax.nn.softmax(top_val, axis=-1)
    A = T * TOP_K
    flat_e = top_idx.reshape(A)
    flat_tok = jnp.arange(A, dtype=jnp.int32) // TOP_K
    flat_gate = gates.reshape(A)
    order = jnp.argsort(flat_e)
    e_sorted = flat_e[order]
    counts = jnp.bincount(flat_e, length=N_EXPERTS).astype(jnp.int32)
    padded = (counts + MOE_BLK - 1) // MOE_BLK * MOE_BLK
    pad_end = jnp.cumsum(padded)
    pad_start = pad_end - padded
    start = jnp.cumsum(counts) - counts
    dest = pad_start[e_sorted] + (jnp.arange(A, dtype=jnp.int32) - start[e_sorted])
    n_blocks = -(-A // MOE_BLK) + N_EXPERTS
    n_slots = n_blocks * MOE_BLK
    slot_tok = jnp.full((n_slots,), T, jnp.int32).at[dest].set(flat_tok[order])
    slot_gate = jnp.zeros((n_slots,), jnp.float32).at[dest].set(flat_gate[order])
    block_start = jnp.arange(n_blocks, dtype=jnp.int32) * MOE_BLK
    block_expert = jnp.minimum(jnp.searchsorted(pad_end, block_start, side='right'), N_EXPERTS - 1)
    h_ext = jnp.concatenate([h, jnp.zeros((1, D), h.dtype)], axis=0)
    xs = h_ext[slot_tok].reshape(n_blocks, MOE_BLK, D)

    def expert_block(args):
        xb, e = args
        a = xb @ w_mlp1[e] + b_mlp1[e]
        return _clamped_swiglu(a) @ w_mlp2[e] + b_mlp2[e]

    ys = lax.map(expert_block, (xs, block_expert)).reshape(n_slots, D)
    ys = ys * slot_gate[:, None].astype(ys.dtype)
    return jax.ops.segment_sum(ys, slot_tok, num_segments=T + 1)[:T]


def setup_inputs(seed: int = 0) -> dict:
    key = jax.random.key(seed)
    ks = jax.random.split(key, 24)
    f32 = jnp.float32

    def nrm(k, shape, s):
        return jax.random.normal(k, shape, f32) * s

    def gain(k, shape):
        return 1.0 + 0.05 * jax.random.normal(k, shape, f32)

    return {
        "x": nrm(ks[0], (BATCH, SEQ, D_MODEL), 1.0),
        "meta_tokens": nrm(ks[1], (N_META, D_MODEL), 1.0),
        "g_attn": gain(ks[2], (DEPTH, D_MODEL)),
        "w_in": nrm(ks[3], (DEPTH, D_MODEL, IN_WIDTH), D_MODEL ** -0.5),
        "g_q_swa": gain(ks[4], (DEPTH, SWA_DH)),
        "g_k_swa": gain(ks[5], (DEPTH, SWA_DH)),
        "g_q_diff": gain(ks[6], (DEPTH, DIFF_DQ)),
        "g_k_diff": gain(ks[7], (DEPTH, DIFF_DQ)),
        "sink_swa": nrm(ks[8], (DEPTH, SWA_HQ), 0.5),
        "lambda_q1": nrm(ks[9], (DEPTH, DIFF_DQ), 0.1),
        "lambda_k1": nrm(ks[10], (DEPTH, DIFF_DQ), 0.1),
        "lambda_q2": nrm(ks[11], (DEPTH, DIFF_DQ), 0.1),
        "lambda_k2": nrm(ks[12], (DEPTH, DIFF_DQ), 0.1),
        "g_diff_head": gain(ks[13], (DEPTH, DIFF_DV)),
        "w_out": nrm(ks[14], (DEPTH, MIX_WIDTH, D_MODEL), MIX_WIDTH ** -0.5),
        "g_ffn": gain(ks[15], (DEPTH, D_MODEL)),
        "w_router": nrm(ks[16], (DEPTH, D_MODEL, N_EXPERTS), D_MODEL ** -0.5),
        "b_router": nrm(ks[17], (DEPTH, N_EXPERTS), 0.01),
        "w_mlp1": nrm(ks[18], (DEPTH, N_EXPERTS, D_MODEL, 2 * D_FF), D_MODEL ** -0.5),
        "b_mlp1": nrm(ks[19], (DEPTH, N_EXPERTS, 2 * D_FF), 0.02),
        "w_mlp2": nrm(ks[20], (DEPTH, N_EXPERTS, D_FF, D_MODEL), D_FF ** -0.5),
        "b_mlp2": nrm(ks[21], (DEPTH, N_EXPERTS, D_MODEL), 0.02),
    }


def reference(x, meta_tokens, g_attn, w_in, g_q_swa, g_k_swa, g_q_diff, g_k_diff, sink_swa,
              lambda_q1, lambda_k1, lambda_q2, lambda_k2, g_diff_head, w_out, g_ffn,
              w_router, b_router, w_mlp1, b_mlp1, w_mlp2, b_mlp2):
    B = x.shape[0]
    meta = jnp.broadcast_to(meta_tokens.astype(x.dtype)[None], (B, N_META, D_MODEL))
    h = jnp.concatenate([meta, x], axis=1)
    L = h.shape[1]
    pos = jnp.arange(L, dtype=jnp.int32)
    offs = [0]
    for w in IN_WIDTHS:
        offs.append(offs[-1] + w)
    for l in range(DEPTH):
        lambda_init = 0.8 - 0.6 * math.exp(-0.3 * l)
        u = _rmsnorm(h, g_attn[l])
        proj = u @ w_in[l]
        qa, ka, va, qd, kd, vd = [proj[..., offs[i]:offs[i + 1]] for i in range(6)]
        qa = _rope(_rmsnorm(qa.reshape(B, L, SWA_HQ, SWA_DH), g_q_swa[l]), pos)
        ka = _rope(_rmsnorm(ka.reshape(B, L, SWA_HKV, SWA_DH), g_k_swa[l]), pos)
        va = va.reshape(B, L, SWA_HKV, SWA_DH)
        out_a = _swa_sink_attention(qa, ka, va, sink_swa[l], pos)

        qd = qd.reshape(B, L, DIFF_H, 2, DIFF_DQ)
        kd = kd.reshape(B, L, DIFF_H, 2, DIFF_DQ)
        q1 = _rope(_rmsnorm(qd[..., 0, :], g_q_diff[l]), pos)
        q2 = _rope(_rmsnorm(qd[..., 1, :], g_q_diff[l]), pos)
        k1 = _rope(_rmsnorm(kd[..., 0, :], g_k_diff[l]), pos)
        k2 = _rope(_rmsnorm(kd[..., 1, :], g_k_diff[l]), pos)
        vd = vd.reshape(B, L, DIFF_H, DIFF_DV)
        lam = (jnp.exp(jnp.sum(lambda_q1[l].astype(jnp.float32) * lambda_k1[l].astype(jnp.float32)))
               - jnp.exp(jnp.sum(lambda_q2[l].astype(jnp.float32) * lambda_k2[l].astype(jnp.float32)))
               + lambda_init)
        od = _diff_attention(q1, q2, k1, k2, vd, lam)
        out_b = (_rmsnorm(od, g_diff_head[l], 1e-5) * (1.0 - lambda_init)).reshape(B, L, DIFF_H * DIFF_DV)

        mixed = jnp.concatenate([out_a, out_b], axis=-1)
        h = h + mixed @ w_out[l]
        v = _rmsnorm(h, g_ffn[l]).reshape(B * L, D_MODEL)
        y = _moe(v, w_router[l], b_router[l], w_mlp1[l], b_mlp1[l], w_mlp2[l], b_mlp2[l])
        h = h + y.reshape(B, L, D_MODEL)
    return h[:, N_META:]
```

```python
import functools
import math

import numpy as np
import jax
import jax.numpy as jnp
from jax import lax
from jax.experimental import pallas as pl
from jax.experimental.pallas import tpu as pltpu

F32 = jnp.float32
BF16 = jnp.bfloat16

N_META = 16
WINDOW = 128
ROPE_THETA = 10000.0
EPS = 1e-6
NEG_INF = -1e30
SWA_HQ, SWA_HKV, SWA_DH = 8, 2, 64
DIFF_H, DIFF_DQ, DIFF_DV = 8, 32, 64
N_EXPERTS = 32
TOP_K = 4
SWIGLU_LIMIT = 7.0
SWIGLU_ALPHA = 1.702
LOG2E = math.log2(math.e)
LANES = 128
ROW_TILE = (8, LANES)
VMEM_LIMIT = 56 * 1024 * 1024

G_QA, G_KA, G_VA, G_QB, G_KB, G_VB = 0, 4, 6, 8, 12, 16
N_GROUPS = 20
PROJ_W = N_GROUPS * LANES


def _nt_dot(a, b):
    return lax.dot_general(a, b, (((1,), (1,)), ((), ())), preferred_element_type=F32)


def _cparams(sem):
    return pltpu.CompilerParams(dimension_semantics=sem, vmem_limit_bytes=VMEM_LIMIT)


def _inproj_kernel(x_ref, g_ref, w_ref, tab_ref, ma_ref, mb_ref, o_ref):
    x = x_ref[...]
    ms = jnp.mean(x * x, axis=-1, keepdims=True)
    u = (x * lax.rsqrt(ms + EPS) * g_ref[...]).astype(BF16)

    def rope_group(y, mat, inv_dh, c, s):
        z = jnp.concatenate([y.astype(BF16), (y * y).astype(BF16)], axis=1)
        r = jnp.dot(z, mat, preferred_element_type=F32)
        sw, ss = r[:, :LANES], r[:, LANES:]
        return lax.rsqrt(ss * inv_dh + EPS) * (y * c + sw * s)

    chunk = 4 * LANES
    for c0 in range(0, N_GROUPS, 4):
        y4 = jnp.dot(u, w_ref[:, c0 * LANES:c0 * LANES + chunk], preferred_element_type=F32)
        for j in range(4):
            grp = c0 + j
            y = y4[:, j * LANES:(j + 1) * LANES]
            if G_QA <= grp < G_KA:
                y = rope_group(y, ma_ref[...], 1.0 / SWA_DH, tab_ref[0], tab_ref[1])
            elif G_KA <= grp < G_VA:
                y = rope_group(y, ma_ref[...], 1.0 / SWA_DH, tab_ref[2], tab_ref[3])
            elif G_QB <= grp < G_KB:
                y = rope_group(y, mb_ref[...], 1.0 / DIFF_DQ, tab_ref[4], tab_ref[5])
            elif G_KB <= grp < G_VB:
                y = rope_group(y, mb_ref[...], 1.0 / DIFF_DQ, tab_ref[6], tab_ref[7])
            o_ref[:, grp * LANES:(grp + 1) * LANES] = y.astype(BF16)


def _inproj(x2, g, w, tab, mat_a, mat_b, tm):
    rows, d = x2.shape
    n_tab = tab.shape[1] // tm
    return pl.pallas_call(
        _inproj_kernel,
        grid=(rows // tm,),
        in_specs=[
            pl.BlockSpec((tm, d), lambda i: (i, 0)),
            pl.BlockSpec((1, d), lambda i: (0, 0)),
            pl.BlockSpec((d, PROJ_W), lambda i: (0, 0)),
            pl.BlockSpec((8, tm, LANES), lambda i: (0, i % n_tab, 0)),
            pl.BlockSpec((2 * LANES, 2 * LANES), lambda i: (0, 0)),
            pl.BlockSpec((2 * LANES, 2 * LANES), lambda i: (0, 0)),
        ],
        out_specs=pl.BlockSpec((tm, PROJ_W), lambda i: (i, 0)),
        out_shape=jax.ShapeDtypeStruct((rows, PROJ_W), BF16),
        compiler_params=_cparams(("parallel",)), name="inproj",
    )(x2, g, w, tab, mat_a, mat_b)


def _swa_kernel(sink_ref, q_ref, k_ref, v_ref, km_ref, vm_ref, o_ref, *, tq, win, seq):
    i = pl.program_id(1)
    t0 = i * tq
    start = pl.multiple_of(jnp.clip(t0 - WINDOW, 0, seq - win), LANES)
    qpos = t0 + lax.broadcasted_iota(jnp.int32, (tq, win), 0)
    kpos = start + lax.broadcasted_iota(jnp.int32, (tq, win), 1)
    ok = jnp.abs(qpos - kpos) <= WINDOW
    low_half = lax.broadcasted_iota(jnp.int32, (tq, LANES), 1) < SWA_DH
    for hk in range(SWA_HKV):
        cols = slice(hk * LANES, (hk + 1) * LANES)
        kw = k_ref[pl.ds(start, win), cols]
        vw = v_ref[pl.ds(start, win), cols]
        km = km_ref[:, cols]
        vm = vm_ref[:, cols]
        for pair in range(2):
            gq = 2 * hk + pair
            qp = q_ref[:, gq * LANES:(gq + 1) * LANES]
            outs = []
            for half in range(2):
                sink = sink_ref[2 * gq + half]
                qm = jnp.where(low_half == (half == 0), qp, jnp.zeros_like(qp))
                s = jnp.where(ok, _nt_dot(qm, kw), NEG_INF)
                sm = _nt_dot(qm, km)
                m = jnp.maximum(jnp.maximum(jnp.max(s, axis=-1, keepdims=True),
                                            jnp.max(sm, axis=-1, keepdims=True)), sink)
                p = jnp.exp2(s - m)
                pm = jnp.exp2(sm - m)
                l = (jnp.sum(p, axis=-1, keepdims=True) + jnp.sum(pm, axis=-1, keepdims=True)
                     + jnp.exp2(sink - m))
                o = (jnp.dot(p.astype(BF16), vw, preferred_element_type=F32)
                     + jnp.dot(pm.astype(BF16), vm, preferred_element_type=F32))
                outs.append(o / l)
            o_ref[:, gq * LANES:(gq + 1) * LANES] = jnp.where(low_half, outs[0], outs[1]).astype(BF16)


def _swa(proj, proj_meta, sink2, tq, win):
    b, seq, _ = proj.shape
    kern = functools.partial(_swa_kernel, tq=tq, win=win, seq=seq)
    return pl.pallas_call(
        kern,
        grid=(b, seq // tq),
        in_specs=[
            pl.BlockSpec(memory_space=pltpu.SMEM),
            pl.BlockSpec((None, tq, 4 * LANES), lambda bi, i: (bi, i, G_QA // 4)),
            pl.BlockSpec((None, seq, 2 * LANES), lambda bi, i: (bi, 0, G_KA // 2)),
            pl.BlockSpec((None, seq, 2 * LANES), lambda bi, i: (bi, 0, G_VA // 2)),
            pl.BlockSpec((N_META, 2 * LANES), lambda bi, i: (0, G_KA // 2)),
            pl.BlockSpec((N_META, 2 * LANES), lambda bi, i: (0, G_VA // 2)),
        ],
        out_specs=pl.BlockSpec((None, tq, 4 * LANES), lambda bi, i: (bi, i, 0)),
        out_shape=jax.ShapeDtypeStruct((b, seq, 4 * LANES), BF16),
        compiler_params=_cparams(("parallel", "parallel")), name="swa",
    )(sink2, proj, proj, proj, proj_meta, proj_meta)


def _diff_kernel(lam_ref, q_ref, k_ref, v_ref, km_ref, vm_ref, gh_ref, o_ref, m_ref, acc_ref,
                 *, tq, tk, seq):
    q = q_ref[...]
    lane = lax.broadcasted_iota(jnp.int32, (tq, LANES), 1)
    qms = [jnp.where((lane >= DIFF_DQ * c) & (lane < DIFF_DQ * (c + 1)), q, jnp.zeros_like(q))
           for c in range(4)]

    def with_ones(v):
        return jnp.concatenate([v, jnp.ones_like(v)], axis=1)

    km = km_ref[...]
    vm = with_ones(vm_ref[...])
    for c in range(4):
        s = _nt_dot(qms[c], km)
        m = jnp.max(s, axis=-1, keepdims=True)
        p = jnp.exp2(s - m)
        m_ref[c] = m
        acc_ref[c] = jnp.dot(p.astype(BF16), vm, preferred_element_type=F32)

    def body(j, carry):
        off = pl.multiple_of(j * tk, tk)
        kt = k_ref[pl.ds(off, tk), :]
        vt = with_ones(v_ref[pl.ds(off, tk), :])
        for c in range(4):
            s = _nt_dot(qms[c], kt)
            m_old = m_ref[c]
            m_new = jnp.maximum(m_old, jnp.max(s, axis=-1, keepdims=True))
            p = jnp.exp2(s - m_new)
            acc_ref[c] = (jnp.exp2(m_old - m_new) * acc_ref[c]
                          + jnp.dot(p.astype(BF16), vt, preferred_element_type=F32))
            m_ref[c] = m_new
        return carry

    lax.fori_loop(0, seq // tk, body, 0)

    lam = lam_ref[0]

    def normalized(c):
        a = acc_ref[c]
        return a[:, :LANES] / a[:, LANES:LANES + 1]

    low = lane < DIFF_DV
    o = jnp.where(low, normalized(0) - lam * normalized(1), normalized(2) - lam * normalized(3))
    o2 = o * o
    ss_lo = jnp.sum(jnp.where(low, o2, 0.0), axis=-1, keepdims=True)
    ss_hi = jnp.sum(jnp.where(low, 0.0, o2), axis=-1, keepdims=True)
    rs = lax.rsqrt(jnp.where(low, ss_lo, ss_hi) * (1.0 / DIFF_DV) + 1e-5)
    o_ref[...] = (o * rs * gh_ref[...]).astype(BF16)


def _diff(proj, proj_meta, lam, gh, tq, tk):
    b, seq, _ = proj.shape
    n_pairs = DIFF_H // 2
    kern = functools.partial(_diff_kernel, tq=tq, tk=tk, seq=seq)
    return pl.pallas_call(
        kern,
        grid=(b, n_pairs, seq // tq),
        in_specs=[
            pl.BlockSpec(memory_space=pltpu.SMEM),
            pl.BlockSpec((None, tq, LANES), lambda bi, hp, i: (bi, i, G_QB + hp)),
            pl.BlockSpec((None, seq, LANES), lambda bi, hp, i: (bi, 0, G_KB + hp)),
            pl.BlockSpec((None, seq, LANES), lambda bi, hp, i: (bi, 0, G_VB + hp)),
            pl.BlockSpec((N_META, LANES), lambda bi, hp, i: (0, G_KB + hp)),
            pl.BlockSpec((N_META, LANES), lambda bi, hp, i: (0, G_VB + hp)),
            pl.BlockSpec((1, LANES), lambda bi, hp, i: (0, 0)),
        ],
        out_specs=pl.BlockSpec((None, tq, LANES), lambda bi, hp, i: (bi, i, hp)),
        out_shape=jax.ShapeDtypeStruct((b, seq, n_pairs * LANES), BF16),
        scratch_shapes=[pltpu.VMEM((4, tq, 1), F32), pltpu.VMEM((4, tq, 2 * LANES), F32)],
        compiler_params=_cparams(("parallel", "parallel", "parallel")), name="diff",
    )(lam, proj, proj, proj, proj_meta, proj_meta, gh)


def _outproj_kernel(ma_ref, mb_ref, x_ref, wo_ref, g_ref, wrh_ref, wrl_ref, br_ref, tri_ref,
                    h1_ref, v_ref, idx_ref, gate_ref, rank_ref, cnt_ref, carry_ref, *, tm):
    half = wo_ref.shape[0] // 2

    @pl.when(pl.program_id(0) == 0)
    def _():
        carry_ref[...] = jnp.zeros_like(carry_ref)

    h1 = (x_ref[...]
          + jnp.dot(ma_ref[...], wo_ref[:half, :], preferred_element_type=F32)
          + jnp.dot(mb_ref[...], wo_ref[half:, :], preferred_element_type=F32))
    h1_ref[...] = h1
    v = h1 * lax.rsqrt(jnp.mean(h1 * h1, axis=-1, keepdims=True) + EPS) * g_ref[...]
    v_ref[...] = v
    vh = v.astype(BF16)
    vl = (v - vh.astype(F32)).astype(BF16)
    work = (_nt_dot(wrh_ref[...], vh) + _nt_dot(wrl_ref[...], vh) + _nt_dot(wrh_ref[...], vl)
            + br_ref[...])
    iota_e = lax.broadcasted_iota(jnp.int32, (N_EXPERTS, tm), 0)
    vals, idxs, sels = [], [], []
    for _ in range(TOP_K):
        mk = jnp.max(work, axis=0, keepdims=True)
        ik = jnp.min(jnp.where(work == mk, iota_e, N_EXPERTS), axis=0, keepdims=True)
        sel = iota_e == ik
        work = jnp.where(sel, -jnp.inf, work)
        vals.append(mk)
        idxs.append(ik)
        sels.append(sel)
    exps = [jnp.exp(vk - vals[0]) for vk in vals]
    denom = exps[0] + exps[1] + exps[2] + exps[3]
    gate_ref[...] = jnp.concatenate([e / denom for e in exps], axis=0)
    idx_ref[...] = jnp.concatenate(idxs, axis=0)
    cnt = jnp.zeros((N_EXPERTS, tm), F32)
    for sel in sels:
        cnt = cnt + jnp.where(sel, 1.0, 0.0)
    before = jnp.dot(cnt.astype(BF16), tri_ref[...], preferred_element_type=F32) + carry_ref[...]
    ranks = [jnp.sum(jnp.where(sel, before, 0.0), axis=0, keepdims=True) for sel in sels]
    rank_ref[...] = jnp.concatenate(ranks, axis=0).astype(jnp.int32)
    carry_ref[...] = carry_ref[...] + jnp.sum(cnt, axis=1, keepdims=True)
    cnt_ref[...] = jnp.broadcast_to(carry_ref[...], cnt_ref.shape)


def _outproj(mixed_a, mixed_b, x2, wo, g, wrh, wrl, br, tri, tm):
    t, d = x2.shape
    hw = mixed_a.shape[1]
    kern = functools.partial(_outproj_kernel, tm=tm)
    row = lambda i: (i, 0)
    fix = lambda i: (0, 0)
    col = lambda i: (0, i)
    return pl.pallas_call(
        kern,
        grid=(t // tm,),
        in_specs=[
            pl.BlockSpec((tm, hw), row), pl.BlockSpec((tm, hw), row), pl.BlockSpec((tm, d), row),
            pl.BlockSpec((d, d), fix), pl.BlockSpec((1, d), fix),
            pl.BlockSpec((N_EXPERTS, d), fix), pl.BlockSpec((N_EXPERTS, d), fix),
            pl.BlockSpec((N_EXPERTS, 1), fix), pl.BlockSpec((tm, tm), fix),
        ],
        out_specs=[
            pl.BlockSpec((tm, d), row), pl.BlockSpec((tm, d), row),
            pl.BlockSpec((TOP_K, tm), col), pl.BlockSpec((TOP_K, tm), col),
            pl.BlockSpec((TOP_K, tm), col), pl.BlockSpec((N_EXPERTS, LANES), fix),
        ],
        out_shape=[
            jax.ShapeDtypeStruct((t, d), F32), jax.ShapeDtypeStruct((t, d), F32),
            jax.ShapeDtypeStruct((TOP_K, t), jnp.int32), jax.ShapeDtypeStruct((TOP_K, t), F32),
            jax.ShapeDtypeStruct((TOP_K, t), jnp.int32),
            jax.ShapeDtypeStruct((N_EXPERTS, LANES), F32),
        ],
        scratch_shapes=[pltpu.VMEM((N_EXPERTS, 1), F32)],
        compiler_params=_cparams(("arbitrary",)), name="outproj",
    )(mixed_a, mixed_b, x2, wo, g, wrh, wrl, br, tri)


def _dispatch_kernel(dest_ref, v_hbm, xs_in_hbm, xs_hbm, sem, *, tmd):
    del xs_in_hbm
    t0 = pl.program_id(0) * tmd

    def issue(t, carry):
        for k in range(TOP_K):
            pltpu.make_async_copy(v_hbm.at[t0 + t], xs_hbm.at[dest_ref[TOP_K * t + k]], sem).start()
        return carry

    lax.fori_loop(0, tmd, issue, 0)
    n = TOP_K * tmd
    pltpu.make_async_copy(xs_hbm.at[pl.ds(0, n)], xs_hbm.at[pl.ds(0, n)], sem).wait()


def _dispatch(dest_flat, v3, xs_zero, tmd):
    t = v3.shape[0]
    kern = functools.partial(_dispatch_kernel, tmd=tmd)
    return pl.pallas_call(
        kern,
        grid=(t // tmd,),
        in_specs=[
            pl.BlockSpec((TOP_K * tmd,), lambda i: (i,), memory_space=pltpu.SMEM),
            pl.BlockSpec(memory_space=pl.ANY),
            pl.BlockSpec(memory_space=pl.ANY),
        ],
        out_specs=pl.BlockSpec(memory_space=pl.ANY),
        out_shape=jax.ShapeDtypeStruct(xs_zero.shape, xs_zero.dtype),
        scratch_shapes=[pltpu.SemaphoreType.DMA],
        input_output_aliases={2: 0},
        compiler_params=_cparams(("arbitrary",)), name="dispatch",
    )(dest_flat, v3, xs_zero)


def _experts_kernel(te_ref, ts_ref, tv_ref, xs_ref, w1_ref, b1_ref, w2_ref, b2_ref, ys_ref, xb_ref):
    del te_ref, ts_ref
    n_chunks = xs_ref.shape[1]
    d_ff = w2_ref.shape[0]

    @pl.when(tv_ref[pl.program_id(0)] > 0)
    def _():
        for c in range(n_chunks):
            xb_ref[:, c * LANES:(c + 1) * LANES] = xs_ref[:, c, :].astype(BF16)
        a = jnp.dot(xb_ref[...], w1_ref[...], preferred_element_type=F32) + b1_ref[...]
        glu = jnp.minimum(a[:, :d_ff], SWIGLU_LIMIT)
        lin = jnp.clip(a[:, d_ff:], -SWIGLU_LIMIT, SWIGLU_LIMIT)
        act = glu * jax.nn.sigmoid(SWIGLU_ALPHA * glu) * (lin + 1.0)
        y = jnp.dot(act.astype(BF16), w2_ref[...], preferred_element_type=F32) + b2_ref[...]
        for c in range(n_chunks):
            ys_ref[:, c, :] = y[:, c * LANES:(c + 1) * LANES]

    @pl.when(tv_ref[pl.program_id(0)] == 0)
    def _():
        ys_ref[...] = jnp.zeros_like(ys_ref)


def _experts(tile_expert, tile_src, tile_valid, xs3, w1, b1, w2, b2, tm):
    n_slots, n_chunks, _ = xs3.shape
    _, d, two_ff = w1.shape
    d_ff = two_ff // 2
    grid_spec = pltpu.PrefetchScalarGridSpec(
        num_scalar_prefetch=3,
        grid=(n_slots // tm,),
        in_specs=[
            pl.BlockSpec((tm, n_chunks, LANES), lambda i, te, ts, tv: (ts[i], 0, 0)),
            pl.BlockSpec((None, d, two_ff), lambda i, te, ts, tv: (te[i], 0, 0)),
            pl.BlockSpec((None, 1, two_ff), lambda i, te, ts, tv: (te[i], 0, 0)),
            pl.BlockSpec((None, d_ff, d), lambda i, te, ts, tv: (te[i], 0, 0)),
            pl.BlockSpec((None, 1, d), lambda i, te, ts, tv: (te[i], 0, 0)),
        ],
        out_specs=pl.BlockSpec((tm, n_chunks, LANES), lambda i, te, ts, tv: (i, 0, 0)),
        scratch_shapes=[pltpu.VMEM((tm, d), BF16)],
    )
    return pl.pallas_call(
        _experts_kernel,
        grid_spec=grid_spec,
        out_shape=jax.ShapeDtypeStruct(xs3.shape, F32),
        compiler_params=_cparams(("arbitrary",)), name="experts",
    )(tile_expert, tile_src, tile_valid, xs3, w1, b1, w2, b2)


def _combine_kernel(dest_ref, dest_next_ref, gate_ref, h1_ref, ys_hbm, o_ref, buf_ref, y_ref, sem,
                    *, tmc):
    i = pl.program_id(0)
    n = pl.num_programs(0)
    n_chunks = y_ref.shape[1]

    def issue(d_ref, slot):
        def one(t, carry):
            for k in range(TOP_K):
                pltpu.make_async_copy(ys_hbm.at[d_ref[TOP_K * t + k]], buf_ref.at[slot, k, t],
                                      sem.at[slot]).start()
            return carry
        lax.fori_loop(0, tmc, one, 0)

    @pl.when(i == 0)
    def _():
        issue(dest_ref, 0)

    slot = i % 2

    @pl.when(i + 1 < n)
    def _():
        issue(dest_next_ref, 1 - slot)

    pltpu.make_async_copy(buf_ref.at[slot], buf_ref.at[slot], sem.at[slot]).wait()

    def mix(t, carry):
        acc = gate_ref[TOP_K * t] * buf_ref[slot, 0, t]
        for k in range(1, TOP_K):
            acc = acc + gate_ref[TOP_K * t + k] * buf_ref[slot, k, t]
        y_ref[t] = acc
        return carry

    lax.fori_loop(0, tmc, mix, 0)
    for c in range(n_chunks):
        o_ref[:, c * LANES:(c + 1) * LANES] = h1_ref[:, c * LANES:(c + 1) * LANES] + y_ref[:, c, :]


def _combine(dest_flat, gate_flat, h1, ys3, tmc):
    t, d = h1.shape
    n_chunks = ys3.shape[1]
    n = t // tmc
    kern = functools.partial(_combine_kernel, tmc=tmc)
    return pl.pallas_call(
        kern,
        grid=(n,),
        in_specs=[
            pl.BlockSpec((TOP_K * tmc,), lambda i: (i,), memory_space=pltpu.SMEM),
            pl.BlockSpec((TOP_K * tmc,), lambda i: (jnp.minimum(i + 1, n - 1),), memory_space=pltpu.SMEM),
            pl.BlockSpec((TOP_K * tmc,), lambda i: (i,), memory_space=pltpu.SMEM),
            pl.BlockSpec((tmc, d), lambda i: (i, 0)),
            pl.BlockSpec(memory_space=pl.ANY),
        ],
        out_specs=pl.BlockSpec((tmc, d), lambda i: (i, 0)),
        out_shape=jax.ShapeDtypeStruct((t, d), F32),
        scratch_shapes=[
            pltpu.VMEM((2, TOP_K, tmc, n_chunks, LANES), F32),
            pltpu.VMEM((tmc, n_chunks, LANES), F32),
            pltpu.SemaphoreType.DMA((2,)),
        ],
        compiler_params=_cparams(("arbitrary",)), name="combine",
    )(dest_flat, dest_flat, gate_flat, h1, ys3)


def _swap_halves(g, dh):
    return jnp.concatenate([g[dh // 2:], g[:dh // 2]])


def _rope_tables(pos, dh, gq, gk, q_scale):
    inv = ROPE_THETA ** (-jnp.arange(0, dh, 2, dtype=F32) / dh)
    ang = pos.astype(F32)[:, None] * inv[None, :]
    cos = jnp.concatenate([jnp.cos(ang), jnp.cos(ang)], axis=1)
    sin = jnp.concatenate([-jnp.sin(ang), jnp.sin(ang)], axis=1)
    rep = LANES // dh
    tile = lambda a: jnp.tile(a, (1, rep))
    gq, gk = gq.astype(F32), gk.astype(F32)
    return jnp.stack([
        tile(cos * gq[None]) * q_scale, tile(sin * _swap_halves(gq, dh)[None]) * q_scale,
        tile(cos * gk[None]), tile(sin * _swap_halves(gk, dh)[None]),
    ])


def _rope_matrix(dh):
    j = np.arange(LANES)
    swap = (j // dh) * dh + (j % dh + dh // 2) % dh
    m = np.zeros((2 * LANES, 2 * LANES), np.float32)
    m[swap, j] = 1.0
    m[LANES:, LANES:] = (j[:, None] // dh == j[None, :] // dh)
    return jnp.asarray(m, BF16)


def _pick_tile(n, pref):
    t = min(n, pref)
    while n % t:
        t //= 2
    return t


def kernel(x, meta_tokens, g_attn, w_in, g_q_swa, g_k_swa, g_q_diff, g_k_diff, sink_swa, lambda_q1, lambda_k1, lambda_q2, lambda_k2, g_diff_head, w_out, g_ffn, w_router, b_router, w_mlp1, b_mlp1, w_mlp2, b_mlp2):
    b, seq, d = x.shape
    assert g_attn.shape[0] == 1 and d % LANES == 0 and seq % 512 == 0
    t = b * seq
    lambda_init = 0.8 - 0.6 * math.exp(-0.3 * 0)

    wi = w_in[0]
    q_a, k_a, v_a, q_b, k_b, v_b = (wi[:, s:e] for s, e in
                                    ((0, 512), (512, 640), (640, 768), (768, 1280), (1280, 1792), (1792, 2304)))
    dup = lambda w: jnp.concatenate([w[:, :64], w[:, :64], w[:, 64:], w[:, 64:]], axis=1)
    w_wide = jnp.concatenate([q_a, dup(k_a), dup(v_a), q_b, k_b, v_b], axis=1).astype(BF16)
    mat_a, mat_b = _rope_matrix(SWA_DH), _rope_matrix(DIFF_DQ)

    def tables(pos):
        return jnp.concatenate([
            _rope_tables(pos, SWA_DH, g_q_swa[0], g_k_swa[0], SWA_DH ** -0.5 * LOG2E),
            _rope_tables(pos, DIFF_DQ, g_q_diff[0], g_k_diff[0], DIFF_DQ ** -0.5 * LOG2E)])

    tab_tok = tables(jnp.arange(N_META, N_META + seq))
    tab_meta = tables(jnp.arange(N_META))
    g_attn2 = g_attn[0].reshape(1, d).astype(F32)

    x2 = x.reshape(t, d)
    tm = _pick_tile(seq, 512)
    proj = _inproj(x2, g_attn2, w_wide, tab_tok, mat_a, mat_b, tm).reshape(b, seq, PROJ_W)
    proj_meta = _inproj(meta_tokens.astype(F32), g_attn2, w_wide, tab_meta, mat_a, mat_b, N_META)

    mixed_a = _swa(proj, proj_meta, sink_swa[0].astype(F32) * LOG2E, 256, 512)
    lam = (jnp.exp(jnp.sum(lambda_q1[0].astype(F32) * lambda_k1[0].astype(F32)))
           - jnp.exp(jnp.sum(lambda_q2[0].astype(F32) * lambda_k2[0].astype(F32))) + lambda_init)
    gh = (jnp.tile(g_diff_head[0].astype(F32), 2) * (1.0 - lambda_init)).reshape(1, LANES)
    mixed_b = _diff(proj, proj_meta, lam.reshape(1), gh, _pick_tile(seq, 512), _pick_tile(seq, 512))

    wr = w_router[0].T.astype(F32)
    wrh = wr.astype(BF16)
    wrl = (wr - wrh.astype(F32)).astype(BF16)
    tri = jnp.asarray(np.triu(np.ones((tm, tm), np.float32), 1), BF16)
    h1, v, idx_t, gate_t, rank_t, cnt = _outproj(
        mixed_a.reshape(t, -1), mixed_b.reshape(t, -1), x2, w_out[0].astype(BF16),
        g_ffn[0].reshape(1, d).astype(F32), wrh, wrl, b_router[0].reshape(N_EXPERTS, 1).astype(F32),
        tri, tm)

    tme = 512
    counts = cnt[:, 0].astype(jnp.int32)
    padded = (counts + tme - 1) // tme * tme
    pad_end = jnp.cumsum(padded)
    pad_start = pad_end - padded
    dest_flat = (pad_start[idx_t] + rank_t).T.reshape(-1)
    gate_flat = gate_t.T.reshape(-1)
    n_tiles = -(-(TOP_K * t) // tme) + N_EXPERTS
    n_slots = n_tiles * tme
    tiles = jnp.arange(n_tiles, dtype=jnp.int32)
    n_valid = pad_end[-1] // tme
    tile_src = jnp.minimum(tiles, n_valid - 1)
    tile_expert = jnp.minimum(jnp.searchsorted(pad_end, tile_src * tme, side='right'),
                              N_EXPERTS - 1).astype(jnp.int32)
    tile_valid = (tiles < n_valid).astype(jnp.int32)

    n_chunks = d // LANES
    v3 = v.reshape(t, n_chunks, LANES)
    xs3 = _dispatch(dest_flat, v3, jnp.zeros((n_slots, n_chunks, LANES), F32), _pick_tile(t, 1024))
    ys3 = _experts(tile_expert, tile_src, tile_valid, xs3,
                   w_mlp1[0].astype(BF16), b_mlp1[0].reshape(N_EXPERTS, 1, -1).astype(F32),
                   w_mlp2[0].astype(BF16), b_mlp2[0].reshape(N_EXPERTS, 1, -1).astype(F32), tme)
    out = _combine(dest_flat, gate_flat, h1, ys3, _pick_tile(t, 256))
    return out.reshape(b, seq, d)
```

```python
import functools
import math

import numpy as np
import jax
import jax.numpy as jnp
from jax import lax
from jax.experimental import pallas as pl
from jax.experimental.pallas import tpu as pltpu

F32 = jnp.float32
BF16 = jnp.bfloat16

N_META = 16
WINDOW = 128
ROPE_THETA = 10000.0
EPS = 1e-6
NEG_INF = -1e30
SWA_HQ, SWA_HKV, SWA_DH = 8, 2, 64
DIFF_H, DIFF_DQ, DIFF_DV = 8, 32, 64
N_EXPERTS = 32
TOP_K = 4
SWIGLU_LIMIT = 7.0
SWIGLU_ALPHA = 1.702
LOG2E = math.log2(math.e)
LANES = 128
ROW_TILE = (8, LANES)
VMEM_LIMIT = 56 * 1024 * 1024
MAX_SHIFT_BOUND = 60.0

G_QA, G_KA, G_VA, G_QB, G_KB, G_VB = 0, 4, 6, 8, 12, 16
N_GROUPS = 20
PROJ_W = N_GROUPS * LANES


def _nt_dot(a, b):
    return lax.dot_general(a, b, (((1,), (1,)), ((), ())), preferred_element_type=F32)


def _cparams(sem):
    return pltpu.CompilerParams(dimension_semantics=sem, vmem_limit_bytes=VMEM_LIMIT)


def _inproj_kernel(x_ref, g_ref, w_ref, tab_ref, ma_ref, mb_ref, o_ref):
    x = x_ref[...]
    ms = jnp.mean(x * x, axis=-1, keepdims=True)
    u = (x * lax.rsqrt(ms + EPS) * g_ref[...]).astype(BF16)

    def rope_group(y, mat, inv_dh, c, s):
        z = jnp.concatenate([y.astype(BF16), (y * y).astype(BF16)], axis=1)
        r = jnp.dot(z, mat, preferred_element_type=F32)
        sw, ss = r[:, :LANES], r[:, LANES:]
        return lax.rsqrt(ss * inv_dh + EPS) * (y * c + sw * s)

    chunk = 4 * LANES
    for c0 in range(0, N_GROUPS, 4):
        y4 = jnp.dot(u, w_ref[:, c0 * LANES:c0 * LANES + chunk], preferred_element_type=F32)
        for j in range(4):
            grp = c0 + j
            y = y4[:, j * LANES:(j + 1) * LANES]
            if G_QA <= grp < G_KA:
                y = rope_group(y, ma_ref[...], 1.0 / SWA_DH, tab_ref[0], tab_ref[1])
            elif G_KA <= grp < G_VA:
                y = rope_group(y, ma_ref[...], 1.0 / SWA_DH, tab_ref[2], tab_ref[3])
            elif G_QB <= grp < G_KB:
                y = rope_group(y, mb_ref[...], 1.0 / DIFF_DQ, tab_ref[4], tab_ref[5])
            elif G_KB <= grp < G_VB:
                y = rope_group(y, mb_ref[...], 1.0 / DIFF_DQ, tab_ref[6], tab_ref[7])
            o_ref[:, grp * LANES:(grp + 1) * LANES] = y.astype(BF16)


def _inproj(x2, g, w, tab, mat_a, mat_b, tm):
    rows, d = x2.shape
    n_tab = tab.shape[1] // tm
    return pl.pallas_call(
        _inproj_kernel,
        grid=(rows // tm,),
        in_specs=[
            pl.BlockSpec((tm, d), lambda i: (i, 0)),
            pl.BlockSpec((1, d), lambda i: (0, 0)),
            pl.BlockSpec((d, PROJ_W), lambda i: (0, 0)),
            pl.BlockSpec((8, tm, LANES), lambda i: (0, i % n_tab, 0)),
            pl.BlockSpec((2 * LANES, 2 * LANES), lambda i: (0, 0)),
            pl.BlockSpec((2 * LANES, 2 * LANES), lambda i: (0, 0)),
        ],
        out_specs=pl.BlockSpec((tm, PROJ_W), lambda i: (i, 0)),
        out_shape=jax.ShapeDtypeStruct((rows, PROJ_W), BF16),
        compiler_params=_cparams(("parallel",)), name="inproj",
    )(x2, g, w, tab, mat_a, mat_b)


def _swa_kernel(sink_ref, q_ref, k_ref, v_ref, km_ref, vm_ref, o_ref, *, tq, win, seq):
    i = pl.program_id(1)
    t0 = i * tq
    start = pl.multiple_of(jnp.clip(t0 - WINDOW, 0, seq - win), LANES)
    qpos = t0 + lax.broadcasted_iota(jnp.int32, (tq, win), 0)
    kpos = start + lax.broadcasted_iota(jnp.int32, (tq, win), 1)
    ok = jnp.abs(qpos - kpos) <= WINDOW
    low_half = lax.broadcasted_iota(jnp.int32, (tq, LANES), 1) < SWA_DH
    for hk in range(SWA_HKV):
        cols = slice(hk * LANES, (hk + 1) * LANES)
        kw = k_ref[pl.ds(start, win), cols]
        vw = v_ref[pl.ds(start, win), cols]
        km = km_ref[:, cols]
        vm = vm_ref[:, cols]
        for pair in range(2):
            gq = 2 * hk + pair
            qp = q_ref[:, gq * LANES:(gq + 1) * LANES]
            outs = []
            for half in range(2):
                sink = sink_ref[2 * gq + half]
                qm = jnp.where(low_half == (half == 0), qp, jnp.zeros_like(qp))
                s = jnp.where(ok, _nt_dot(qm, kw), NEG_INF)
                sm = _nt_dot(qm, km)
                m = jnp.maximum(jnp.maximum(jnp.max(s, axis=-1, keepdims=True),
                                            jnp.max(sm, axis=-1, keepdims=True)), sink)
                p = jnp.exp2(s - m)
                pm = jnp.exp2(sm - m)
                l = (jnp.sum(p, axis=-1, keepdims=True) + jnp.sum(pm, axis=-1, keepdims=True)
                     + jnp.exp2(sink - m))
                o = (jnp.dot(p.astype(BF16), vw, preferred_element_type=F32)
                     + jnp.dot(pm.astype(BF16), vm, preferred_element_type=F32))
                outs.append(o / l)
            o_ref[:, gq * LANES:(gq + 1) * LANES] = jnp.where(low_half, outs[0], outs[1]).astype(BF16)


def _swa(proj, proj_meta, sink2, tq, win):
    b, seq, _ = proj.shape
    kern = functools.partial(_swa_kernel, tq=tq, win=win, seq=seq)
    return pl.pallas_call(
        kern,
        grid=(b, seq // tq),
        in_specs=[
            pl.BlockSpec(memory_space=pltpu.SMEM),
            pl.BlockSpec((None, tq, 4 * LANES), lambda bi, i: (bi, i, G_QA // 4)),
            pl.BlockSpec((None, seq, 2 * LANES), lambda bi, i: (bi, 0, G_KA // 2)),
            pl.BlockSpec((None, seq, 2 * LANES), lambda bi, i: (bi, 0, G_VA // 2)),
            pl.BlockSpec((N_META, 2 * LANES), lambda bi, i: (0, G_KA // 2)),
            pl.BlockSpec((N_META, 2 * LANES), lambda bi, i: (0, G_VA // 2)),
        ],
        out_specs=pl.BlockSpec((None, tq, 4 * LANES), lambda bi, i: (bi, i, 0)),
        out_shape=jax.ShapeDtypeStruct((b, seq, 4 * LANES), BF16),
        compiler_params=_cparams(("parallel", "parallel")), name="swa",
    )(sink2, proj, proj, proj, proj_meta, proj_meta)


def _quarter_masked(q):
    lane = lax.broadcasted_iota(jnp.int32, q.shape, 1)
    return [jnp.where((lane >= DIFF_DQ * c) & (lane < DIFF_DQ * (c + 1)), q, jnp.zeros_like(q))
            for c in range(4)]


def _with_ones(v):
    return jnp.concatenate([v, jnp.ones_like(v)], axis=1)


def _diff_finish(accs, lam, gh, o_ref):
    def normalized(a):
        return a[:, :LANES] / a[:, LANES:LANES + 1]

    low = lax.broadcasted_iota(jnp.int32, (accs[0].shape[0], LANES), 1) < DIFF_DV
    o = jnp.where(low, normalized(accs[0]) - lam * normalized(accs[1]),
                  normalized(accs[2]) - lam * normalized(accs[3]))
    o2 = o * o
    ss_lo = jnp.sum(jnp.where(low, o2, 0.0), axis=-1, keepdims=True)
    ss_hi = jnp.sum(jnp.where(low, 0.0, o2), axis=-1, keepdims=True)
    rs = lax.rsqrt(jnp.where(low, ss_lo, ss_hi) * (1.0 / DIFF_DV) + 1e-5)
    o_ref[...] = (o * rs * gh).astype(BF16)


def _diff_bounded_kernel(sc_ref, q_ref, k_ref, v_ref, km_ref, vm_ref, gh_ref, o_ref, *, tq, tk, seq):
    bound = sc_ref[1]
    q4 = jnp.concatenate(_quarter_masked(q_ref[...]), axis=0)
    pm = jnp.exp2(_nt_dot(q4, km_ref[...]) - bound).astype(BF16)
    acc = jnp.dot(pm, _with_ones(vm_ref[...]), preferred_element_type=F32)
    for j in range(seq // tk):
        p = jnp.exp2(_nt_dot(q4, k_ref[j * tk:(j + 1) * tk, :]) - bound).astype(BF16)
        acc = acc + jnp.dot(p, _with_ones(v_ref[j * tk:(j + 1) * tk, :]), preferred_element_type=F32)
    _diff_finish([acc[c * tq:(c + 1) * tq] for c in range(4)], sc_ref[0], gh_ref[...], o_ref)


def _diff_online_kernel(sc_ref, q_ref, k_ref, v_ref, km_ref, vm_ref, gh_ref, o_ref, m_ref, acc_ref,
                        *, tq, tk, seq):
    qms = _quarter_masked(q_ref[...])
    km = km_ref[...]
    vm = _with_ones(vm_ref[...])
    for c in range(4):
        s = _nt_dot(qms[c], km)
        m = jnp.max(s, axis=-1, keepdims=True)
        p = jnp.exp2(s - m)
        m_ref[c] = m
        acc_ref[c] = jnp.dot(p.astype(BF16), vm, preferred_element_type=F32)

    def body(j, carry):
        off = pl.multiple_of(j * tk, tk)
        kt = k_ref[pl.ds(off, tk), :]
        vt = _with_ones(v_ref[pl.ds(off, tk), :])
        for c in range(4):
            s = _nt_dot(qms[c], kt)
            m_old = m_ref[c]
            m_new = jnp.maximum(m_old, jnp.max(s, axis=-1, keepdims=True))
            p = jnp.exp2(s - m_new)
            acc_ref[c] = (jnp.exp2(m_old - m_new) * acc_ref[c]
                          + jnp.dot(p.astype(BF16), vt, preferred_element_type=F32))
            m_ref[c] = m_new
        return carry

    lax.fori_loop(0, seq // tk, body, 0)
    _diff_finish([acc_ref[c] for c in range(4)], sc_ref[0], gh_ref[...], o_ref)


def _diff(proj, proj_meta, scalars, gh, tq, tk, bounded):
    b, seq, _ = proj.shape
    n_pairs = DIFF_H // 2
    if bounded:
        kern = functools.partial(_diff_bounded_kernel, tq=tq, tk=tk, seq=seq)
        scratch = []
    else:
        kern = functools.partial(_diff_online_kernel, tq=tq, tk=tk, seq=seq)
        scratch = [pltpu.VMEM((4, tq, 1), F32), pltpu.VMEM((4, tq, 2 * LANES), F32)]
    return pl.pallas_call(
        kern,
        grid=(b, n_pairs, seq // tq),
        in_specs=[
            pl.BlockSpec(memory_space=pltpu.SMEM),
            pl.BlockSpec((None, tq, LANES), lambda bi, hp, i: (bi, i, G_QB + hp)),
            pl.BlockSpec((None, seq, LANES), lambda bi, hp, i: (bi, 0, G_KB + hp)),
            pl.BlockSpec((None, seq, LANES), lambda bi, hp, i: (bi, 0, G_VB + hp)),
            pl.BlockSpec((N_META, LANES), lambda bi, hp, i: (0, G_KB + hp)),
            pl.BlockSpec((N_META, LANES), lambda bi, hp, i: (0, G_VB + hp)),
            pl.BlockSpec((1, LANES), lambda bi, hp, i: (0, 0)),
        ],
        out_specs=pl.BlockSpec((None, tq, LANES), lambda bi, hp, i: (bi, i, hp)),
        out_shape=jax.ShapeDtypeStruct((b, seq, n_pairs * LANES), BF16),
        scratch_shapes=scratch,
        compiler_params=_cparams(("parallel", "parallel", "parallel")),
        name="diff_bounded" if bounded else "diff_online",
    )(scalars, proj, proj, proj, proj_meta, proj_meta, gh)


def _outproj_kernel(ma_ref, mb_ref, x_ref, wo_ref, g_ref, wrh_ref, wrl_ref, br_ref, tri_ref,
                    h1_ref, v_ref, idx_ref, gate_ref, rank_ref, cnt_ref, carry_ref, *, tm):
    half = wo_ref.shape[0] // 2

    @pl.when(pl.program_id(0) == 0)
    def _():
        carry_ref[...] = jnp.zeros_like(carry_ref)

    h1 = (x_ref[...]
          + jnp.dot(ma_ref[...], wo_ref[:half, :], preferred_element_type=F32)
          + jnp.dot(mb_ref[...], wo_ref[half:, :], preferred_element_type=F32))
    h1_ref[...] = h1
    v = h1 * lax.rsqrt(jnp.mean(h1 * h1, axis=-1, keepdims=True) + EPS) * g_ref[...]
    for c in range(v_ref.shape[1]):
        v_ref[:, c, :] = v[:, c * LANES:(c + 1) * LANES]
    vh = v.astype(BF16)
    vl = (v - vh.astype(F32)).astype(BF16)
    work = (_nt_dot(wrh_ref[...], vh) + _nt_dot(wrl_ref[...], vh) + _nt_dot(wrh_ref[...], vl)
            + br_ref[...])
    iota_e = lax.broadcasted_iota(jnp.int32, (N_EXPERTS, tm), 0)
    vals, idxs, sels = [], [], []
    for _ in range(TOP_K):
        mk = jnp.max(work, axis=0, keepdims=True)
        ik = jnp.min(jnp.where(work == mk, iota_e, N_EXPERTS), axis=0, keepdims=True)
        sel = iota_e == ik
        work = jnp.where(sel, -jnp.inf, work)
        vals.append(mk)
        idxs.append(ik)
        sels.append(sel)
    exps = [jnp.exp(vk - vals[0]) for vk in vals]
    denom = exps[0] + exps[1] + exps[2] + exps[3]
    gate_ref[...] = jnp.concatenate([e / denom for e in exps], axis=0)
    idx_ref[...] = jnp.concatenate(idxs, axis=0)
    cnt = jnp.zeros((N_EXPERTS, tm), F32)
    for sel in sels:
        cnt = cnt + jnp.where(sel, 1.0, 0.0)
    before = jnp.dot(cnt.astype(BF16), tri_ref[...], preferred_element_type=F32) + carry_ref[...]
    ranks = [jnp.sum(jnp.where(sel, before, 0.0), axis=0, keepdims=True) for sel in sels]
    rank_ref[...] = jnp.concatenate(ranks, axis=0).astype(jnp.int32)
    carry_ref[...] = carry_ref[...] + jnp.sum(cnt, axis=1, keepdims=True)
    cnt_ref[...] = jnp.broadcast_to(carry_ref[...], cnt_ref.shape)


def _outproj(mixed_a, mixed_b, x2, wo, g, wrh, wrl, br, tri, tm):
    t, d = x2.shape
    hw = mixed_a.shape[1]
    kern = functools.partial(_outproj_kernel, tm=tm)
    row = lambda i: (i, 0)
    fix = lambda i: (0, 0)
    col = lambda i: (0, i)
    return pl.pallas_call(
        kern,
        grid=(t // tm,),
        in_specs=[
            pl.BlockSpec((tm, hw), row), pl.BlockSpec((tm, hw), row), pl.BlockSpec((tm, d), row),
            pl.BlockSpec((d, d), fix), pl.BlockSpec((1, d), fix),
            pl.BlockSpec((N_EXPERTS, d), fix), pl.BlockSpec((N_EXPERTS, d), fix),
            pl.BlockSpec((N_EXPERTS, 1), fix), pl.BlockSpec((tm, tm), fix),
        ],
        out_specs=[
            pl.BlockSpec((tm, d), row), pl.BlockSpec((tm, d // LANES, LANES), lambda i: (i, 0, 0)),
            pl.BlockSpec((TOP_K, tm), col), pl.BlockSpec((TOP_K, tm), col),
            pl.BlockSpec((TOP_K, tm), col), pl.BlockSpec((N_EXPERTS, LANES), fix),
        ],
        out_shape=[
            jax.ShapeDtypeStruct((t, d), F32), jax.ShapeDtypeStruct((t, d // LANES, LANES), F32),
            jax.ShapeDtypeStruct((TOP_K, t), jnp.int32), jax.ShapeDtypeStruct((TOP_K, t), F32),
            jax.ShapeDtypeStruct((TOP_K, t), jnp.int32),
            jax.ShapeDtypeStruct((N_EXPERTS, LANES), F32),
        ],
        scratch_shapes=[pltpu.VMEM((N_EXPERTS, 1), F32)],
        compiler_params=_cparams(("arbitrary",)), name="outproj",
    )(mixed_a, mixed_b, x2, wo, g, wrh, wrl, br, tri)


def _dispatch_kernel(dest_ref, v_ref, xs_in_hbm, xs_hbm, sem, *, tmd):
    del xs_in_hbm

    def issue(t, carry):
        for k in range(TOP_K):
            pltpu.make_async_copy(v_ref.at[t], xs_hbm.at[dest_ref[TOP_K * t + k]], sem).start()
        return carry

    lax.fori_loop(0, tmd, issue, 0, unroll=4)
    for k in range(TOP_K):
        pltpu.make_async_copy(v_ref, xs_hbm.at[pl.ds(0, tmd)], sem).wait()


def _dispatch(dest_flat, v3, xs_zero, tmd):
    t, n_chunks, _ = v3.shape
    kern = functools.partial(_dispatch_kernel, tmd=tmd)
    return pl.pallas_call(
        kern,
        grid=(t // tmd,),
        in_specs=[
            pl.BlockSpec((TOP_K * tmd,), lambda i: (i,), memory_space=pltpu.SMEM),
            pl.BlockSpec((tmd, n_chunks, LANES), lambda i: (i, 0, 0)),
            pl.BlockSpec(memory_space=pl.ANY),
        ],
        out_specs=pl.BlockSpec(memory_space=pl.ANY),
        out_shape=jax.ShapeDtypeStruct(xs_zero.shape, xs_zero.dtype),
        scratch_shapes=[pltpu.SemaphoreType.DMA],
        input_output_aliases={2: 0},
        compiler_params=_cparams(("arbitrary",)), name="dispatch",
    )(dest_flat, v3, xs_zero)


def _experts_kernel(te_ref, ts_ref, tv_ref, xs_ref, w1_ref, b1_ref, w2_ref, b2_ref, ys_ref, xb_ref):
    del te_ref, ts_ref
    n_chunks = xs_ref.shape[1]
    d_ff = w2_ref.shape[0]

    @pl.when(tv_ref[pl.program_id(0)] > 0)
    def _():
        for c in range(n_chunks):
            xb_ref[:, c * LANES:(c + 1) * LANES] = xs_ref[:, c, :].astype(BF16)
        a = jnp.dot(xb_ref[...], w1_ref[...], preferred_element_type=F32) + b1_ref[...]
        glu = jnp.minimum(a[:, :d_ff], SWIGLU_LIMIT)
        lin = jnp.clip(a[:, d_ff:], -SWIGLU_LIMIT, SWIGLU_LIMIT)
        act = glu * jax.nn.sigmoid(SWIGLU_ALPHA * glu) * (lin + 1.0)
        y = jnp.dot(act.astype(BF16), w2_ref[...], preferred_element_type=F32) + b2_ref[...]
        for c in range(n_chunks):
            ys_ref[:, c, :] = y[:, c * LANES:(c + 1) * LANES]

    @pl.when(tv_ref[pl.program_id(0)] == 0)
    def _():
        ys_ref[...] = jnp.zeros_like(ys_ref)


def _experts(tile_expert, tile_src, tile_valid, xs3, w1, b1, w2, b2, tm):
    n_slots, n_chunks, _ = xs3.shape
    _, d, two_ff = w1.shape
    d_ff = two_ff // 2
    grid_spec = pltpu.PrefetchScalarGridSpec(
        num_scalar_prefetch=3,
        grid=(n_slots // tm,),
        in_specs=[
            pl.BlockSpec((tm, n_chunks, LANES), lambda i, te, ts, tv: (ts[i], 0, 0)),
            pl.BlockSpec((None, d, two_ff), lambda i, te, ts, tv: (te[i], 0, 0)),
            pl.BlockSpec((None, 1, two_ff), lambda i, te, ts, tv: (te[i], 0, 0)),
            pl.BlockSpec((None, d_ff, d), lambda i, te, ts, tv: (te[i], 0, 0)),
            pl.BlockSpec((None, 1, d), lambda i, te, ts, tv: (te[i], 0, 0)),
        ],
        out_specs=pl.BlockSpec((tm, n_chunks, LANES), lambda i, te, ts, tv: (i, 0, 0)),
        scratch_shapes=[pltpu.VMEM((tm, d), BF16)],
    )
    return pl.pallas_call(
        _experts_kernel,
        grid_spec=grid_spec,
        out_shape=jax.ShapeDtypeStruct(xs3.shape, F32),
        compiler_params=_cparams(("arbitrary",)), name="experts",
    )(tile_expert, tile_src, tile_valid, xs3, w1, b1, w2, b2)


def _combine_kernel(dest_ref, dest_next_ref, gate_ref, h1_ref, ys_hbm, o_ref, buf_ref, y_ref, sem,
                    *, tmc):
    i = pl.program_id(0)
    n = pl.num_programs(0)
    n_chunks = y_ref.shape[1]

    def issue(d_ref, slot):
        def one(t, carry):
            for k in range(TOP_K):
                pltpu.make_async_copy(ys_hbm.at[d_ref[TOP_K * t + k]], buf_ref.at[slot, k, t],
                                      sem.at[slot]).start()
            return carry
        lax.fori_loop(0, tmc, one, 0)

    @pl.when(i == 0)
    def _():
        issue(dest_ref, 0)

    slot = i % 2

    @pl.when(i + 1 < n)
    def _():
        issue(dest_next_ref, 1 - slot)

    pltpu.make_async_copy(buf_ref.at[slot], buf_ref.at[slot], sem.at[slot]).wait()

    def mix(t, carry):
        acc = gate_ref[TOP_K * t] * buf_ref[slot, 0, t]
        for k in range(1, TOP_K):
            acc = acc + gate_ref[TOP_K * t + k] * buf_ref[slot, k, t]
        y_ref[t] = acc
        return carry

    lax.fori_loop(0, tmc, mix, 0)
    for c in range(n_chunks):
        o_ref[:, c * LANES:(c + 1) * LANES] = h1_ref[:, c * LANES:(c + 1) * LANES] + y_ref[:, c, :]


def _combine(dest_flat, gate_flat, h1, ys3, tmc):
    t, d = h1.shape
    n_chunks = ys3.shape[1]
    n = t // tmc
    kern = functools.partial(_combine_kernel, tmc=tmc)
    return pl.pallas_call(
        kern,
        grid=(n,),
        in_specs=[
            pl.BlockSpec((TOP_K * tmc,), lambda i: (i,), memory_space=pltpu.SMEM),
            pl.BlockSpec((TOP_K * tmc,), lambda i: (jnp.minimum(i + 1, n - 1),), memory_space=pltpu.SMEM),
            pl.BlockSpec((TOP_K * tmc,), lambda i: (i,), memory_space=pltpu.SMEM),
            pl.BlockSpec((tmc, d), lambda i: (i, 0)),
            pl.BlockSpec(memory_space=pl.ANY),
        ],
        out_specs=pl.BlockSpec((tmc, d), lambda i: (i, 0)),
        out_shape=jax.ShapeDtypeStruct((t, d), F32),
        scratch_shapes=[
            pltpu.VMEM((2, TOP_K, tmc, n_chunks, LANES), F32),
            pltpu.VMEM((tmc, n_chunks, LANES), F32),
            pltpu.SemaphoreType.DMA((2,)),
        ],
        compiler_params=_cparams(("arbitrary",)), name="combine",
    )(dest_flat, dest_flat, gate_flat, h1, ys3)


def _swap_halves(g, dh):
    return jnp.concatenate([g[dh // 2:], g[:dh // 2]])


def _rope_tables(pos, dh, gq, gk, q_scale):
    inv = ROPE_THETA ** (-jnp.arange(0, dh, 2, dtype=F32) / dh)
    ang = pos.astype(F32)[:, None] * inv[None, :]
    cos = jnp.concatenate([jnp.cos(ang), jnp.cos(ang)], axis=1)
    sin = jnp.concatenate([-jnp.sin(ang), jnp.sin(ang)], axis=1)
    rep = LANES // dh
    tile = lambda a: jnp.tile(a, (1, rep))
    gq, gk = gq.astype(F32), gk.astype(F32)
    return jnp.stack([
        tile(cos * gq[None]) * q_scale, tile(sin * _swap_halves(gq, dh)[None]) * q_scale,
        tile(cos * gk[None]), tile(sin * _swap_halves(gk, dh)[None]),
    ])


def _rope_matrix(dh):
    j = np.arange(LANES)
    swap = (j // dh) * dh + (j % dh + dh // 2) % dh
    m = np.zeros((2 * LANES, 2 * LANES), np.float32)
    m[swap, j] = 1.0
    m[LANES:, LANES:] = (j[:, None] // dh == j[None, :] // dh)
    return jnp.asarray(m, BF16)


def _pick_tile(n, pref):
    t = min(n, pref)
    while n % t:
        t //= 2
    return t


def kernel(x, meta_tokens, g_attn, w_in, g_q_swa, g_k_swa, g_q_diff, g_k_diff, sink_swa, lambda_q1, lambda_k1, lambda_q2, lambda_k2, g_diff_head, w_out, g_ffn, w_router, b_router, w_mlp1, b_mlp1, w_mlp2, b_mlp2):
    b, seq, d = x.shape
    assert g_attn.shape[0] == 1 and d % LANES == 0 and seq % 512 == 0
    t = b * seq
    lambda_init = 0.8 - 0.6 * math.exp(-0.3 * 0)

    wi = w_in[0]
    q_a, k_a, v_a, q_b, k_b, v_b = (wi[:, s:e] for s, e in
                                    ((0, 512), (512, 640), (640, 768), (768, 1280), (1280, 1792), (1792, 2304)))
    dup = lambda w: jnp.concatenate([w[:, :64], w[:, :64], w[:, 64:], w[:, 64:]], axis=1)
    w_wide = jnp.concatenate([q_a, dup(k_a), dup(v_a), q_b, k_b, v_b], axis=1).astype(BF16)
    mat_a, mat_b = _rope_matrix(SWA_DH), _rope_matrix(DIFF_DQ)

    def tables(pos):
        return jnp.concatenate([
            _rope_tables(pos, SWA_DH, g_q_swa[0], g_k_swa[0], SWA_DH ** -0.5 * LOG2E),
            _rope_tables(pos, DIFF_DQ, g_q_diff[0], g_k_diff[0], DIFF_DQ ** -0.5 * LOG2E)])

    tab_tok = tables(jnp.arange(N_META, N_META + seq))
    tab_meta = tables(jnp.arange(N_META))
    g_attn2 = g_attn[0].reshape(1, d).astype(F32)

    x2 = x.reshape(t, d)
    tm = _pick_tile(seq, 512)
    proj = _inproj(x2, g_attn2, w_wide, tab_tok, mat_a, mat_b, tm).reshape(b, seq, PROJ_W)
    proj_meta = _inproj(meta_tokens.astype(F32), g_attn2, w_wide, tab_meta, mat_a, mat_b, N_META)

    mixed_a = _swa(proj, proj_meta, sink_swa[0].astype(F32) * LOG2E, 256, 512)
    lam = (jnp.exp(jnp.sum(lambda_q1[0].astype(F32) * lambda_k1[0].astype(F32)))
           - jnp.exp(jnp.sum(lambda_q2[0].astype(F32) * lambda_k2[0].astype(F32))) + lambda_init)
    gh = (jnp.tile(g_diff_head[0].astype(F32), 2) * (1.0 - lambda_init)).reshape(1, LANES)
    bound = (DIFF_DQ ** 0.5 * LOG2E * jnp.max(jnp.abs(g_q_diff[0].astype(F32)))
             * jnp.max(jnp.abs(g_k_diff[0].astype(F32))))
    scalars = jnp.stack([lam, bound])
    mixed_b = lax.cond(
        bound <= MAX_SHIFT_BOUND,
        lambda: _diff(proj, proj_meta, scalars, gh, _pick_tile(seq, 128), _pick_tile(seq, 1024), True),
        lambda: _diff(proj, proj_meta, scalars, gh, _pick_tile(seq, 512), _pick_tile(seq, 512), False))

    wr = w_router[0].T.astype(F32)
    wrh = wr.astype(BF16)
    wrl = (wr - wrh.astype(F32)).astype(BF16)
    tri = jnp.asarray(np.triu(np.ones((tm, tm), np.float32), 1), BF16)
    h1, v, idx_t, gate_t, rank_t, cnt = _outproj(
        mixed_a.reshape(t, -1), mixed_b.reshape(t, -1), x2, w_out[0].astype(BF16),
        g_ffn[0].reshape(1, d).astype(F32), wrh, wrl, b_router[0].reshape(N_EXPERTS, 1).astype(F32),
        tri, tm)

    tme = 512
    counts = cnt[:, 0].astype(jnp.int32)
    padded = (counts + tme - 1) // tme * tme
    pad_end = jnp.cumsum(padded)
    pad_start = pad_end - padded
    start_of = jnp.sum(jnp.where(idx_t[None] == jnp.arange(N_EXPERTS)[:, None, None],
                                 pad_start[:, None, None], 0), axis=0)
    dest_flat = (start_of + rank_t).T.reshape(-1)
    gate_flat = gate_t.T.reshape(-1)
    n_tiles = -(-(TOP_K * t) // tme) + N_EXPERTS
    n_slots = n_tiles * tme
    tiles = jnp.arange(n_tiles, dtype=jnp.int32)
    n_valid = pad_end[-1] // tme
    tile_src = jnp.minimum(tiles, n_valid - 1)
    tile_expert = jnp.minimum(jnp.searchsorted(pad_end, tile_src * tme, side='right'),
                              N_EXPERTS - 1).astype(jnp.int32)
    tile_valid = (tiles < n_valid).astype(jnp.int32)

    n_chunks = d // LANES
    xs3 = _dispatch(dest_flat, v, jnp.zeros((n_slots, n_chunks, LANES), F32), _pick_tile(t, 512))
    ys3 = _experts(tile_expert, tile_src, tile_valid, xs3,
                   w_mlp1[0].astype(BF16), b_mlp1[0].reshape(N_EXPERTS, 1, -1).astype(F32),
                   w_mlp2[0].astype(BF16), b_mlp2[0].reshape(N_EXPERTS, 1, -1).astype(F32), tme)
    out = _combine(dest_flat, gate_flat, h1, ys3, _pick_tile(t, 256))
    return out.reshape(b, seq, d)
```

```python
import functools
import math

import numpy as np
import jax
import jax.numpy as jnp
from jax import lax
from jax.experimental import pallas as pl
from jax.experimental.pallas import tpu as pltpu

F32 = jnp.float32
BF16 = jnp.bfloat16

N_META = 16
WINDOW = 128
ROPE_THETA = 10000.0
EPS = 1e-6
NEG_INF = -1e30
SWA_HQ, SWA_HKV, SWA_DH = 8, 2, 64
DIFF_H, DIFF_DQ, DIFF_DV = 8, 32, 64
N_EXPERTS = 32
TOP_K = 4
SWIGLU_LIMIT = 7.0
SWIGLU_ALPHA = 1.702
LOG2E = math.log2(math.e)
LANES = 128
VMEM_LIMIT = 56 * 1024 * 1024
MAX_SHIFT_BOUND = 60.0

G_QA, G_KA, G_VA, G_QB, G_KB, G_VB = 0, 4, 6, 8, 12, 16
N_GROUPS = 20
PROJ_W = N_GROUPS * LANES


def _nt_dot(a, b):
    return lax.dot_general(a, b, (((1,), (1,)), ((), ())), preferred_element_type=F32)


def _cparams(sem):
    return pltpu.CompilerParams(dimension_semantics=sem, vmem_limit_bytes=VMEM_LIMIT)


def _inproj_kernel(x_ref, g_ref, w_ref, tab_ref, ma_ref, mb_ref, o_ref):
    x = x_ref[...]
    ms = jnp.mean(x * x, axis=-1, keepdims=True)
    u = (x * lax.rsqrt(ms + EPS) * g_ref[...]).astype(BF16)

    def rope_group(y, mat, inv_dh, c, s):
        z = jnp.concatenate([y.astype(BF16), (y * y).astype(BF16)], axis=1)
        r = jnp.dot(z, mat, preferred_element_type=F32)
        sw, ss = r[:, :LANES], r[:, LANES:]
        return lax.rsqrt(ss * inv_dh + EPS) * (y * c + sw * s)

    chunk = 4 * LANES
    for c0 in range(0, N_GROUPS, 4):
        y4 = jnp.dot(u, w_ref[:, c0 * LANES:c0 * LANES + chunk], preferred_element_type=F32)
        for j in range(4):
            grp = c0 + j
            y = y4[:, j * LANES:(j + 1) * LANES]
            if G_QA <= grp < G_KA:
                y = rope_group(y, ma_ref[...], 1.0 / SWA_DH, tab_ref[0], tab_ref[1])
            elif G_KA <= grp < G_VA:
                y = rope_group(y, ma_ref[...], 1.0 / SWA_DH, tab_ref[2], tab_ref[3])
            elif G_QB <= grp < G_KB:
                y = rope_group(y, mb_ref[...], 1.0 / DIFF_DQ, tab_ref[4], tab_ref[5])
            elif G_KB <= grp < G_VB:
                y = rope_group(y, mb_ref[...], 1.0 / DIFF_DQ, tab_ref[6], tab_ref[7])
            o_ref[:, grp * LANES:(grp + 1) * LANES] = y.astype(BF16)


def _inproj(x2, g, w, tab, mat_a, mat_b, tm):
    rows, d = x2.shape
    n_tab = tab.shape[1] // tm
    return pl.pallas_call(
        _inproj_kernel,
        grid=(rows // tm,),
        in_specs=[
            pl.BlockSpec((tm, d), lambda i: (i, 0)),
            pl.BlockSpec((1, d), lambda i: (0, 0)),
            pl.BlockSpec((d, PROJ_W), lambda i: (0, 0)),
            pl.BlockSpec((8, tm, LANES), lambda i: (0, i % n_tab, 0)),
            pl.BlockSpec((2 * LANES, 2 * LANES), lambda i: (0, 0)),
            pl.BlockSpec((2 * LANES, 2 * LANES), lambda i: (0, 0)),
        ],
        out_specs=pl.BlockSpec((tm, PROJ_W), lambda i: (i, 0)),
        out_shape=jax.ShapeDtypeStruct((rows, PROJ_W), BF16),
        compiler_params=_cparams(("parallel",)), name="inproj",
    )(x2, g, w, tab, mat_a, mat_b)


def _swa_window(tq, win, seq):
    t0 = pl.program_id(1) * tq
    start = pl.multiple_of(jnp.clip(t0 - WINDOW, 0, seq - win), LANES)
    qpos = t0 + lax.broadcasted_iota(jnp.int32, (tq, win), 0)
    kpos = start + lax.broadcasted_iota(jnp.int32, (tq, win), 1)
    return start, jnp.abs(qpos - kpos) <= WINDOW


def _swa_bounded_kernel(sc_ref, q_ref, k_ref, v_ref, km_ref, vm_ref, o_ref, *, tq, win, seq):
    start, ok = _swa_window(tq, win, seq)
    bound = sc_ref[SWA_HQ]
    low_half = lax.broadcasted_iota(jnp.int32, (tq, LANES), 1) < SWA_DH
    n_rep = SWA_HQ // SWA_HKV
    for hk in range(SWA_HKV):
        cols = slice(hk * LANES, (hk + 1) * LANES)
        qs = []
        for pair in range(2):
            qp = q_ref[:, (2 * hk + pair) * LANES:(2 * hk + pair + 1) * LANES]
            qs += [jnp.where(low_half, qp, jnp.zeros_like(qp)), jnp.where(low_half, jnp.zeros_like(qp), qp)]
        q4 = jnp.concatenate(qs, axis=0)
        s = _nt_dot(q4, k_ref[pl.ds(start, win), cols])
        sm = _nt_dot(q4, km_ref[:, cols])
        ps, pms, sink_p = [], [], []
        for r in range(n_rep):
            sink = sc_ref[n_rep * hk + r]
            shift = jnp.maximum(bound, sink)
            rows = slice(r * tq, (r + 1) * tq)
            ps.append(jnp.exp2(jnp.where(ok, s[rows] - shift, NEG_INF)).astype(BF16))
            pms.append(jnp.exp2(sm[rows] - shift).astype(BF16))
            sink_p.append(jnp.exp2(jnp.full((tq, 1), sink - shift, F32)))
        acc = (jnp.dot(jnp.concatenate(ps, axis=0), _with_ones(v_ref[pl.ds(start, win), cols]),
                       preferred_element_type=F32)
               + jnp.dot(jnp.concatenate(pms, axis=0), _with_ones(vm_ref[:, cols]),
                         preferred_element_type=F32))
        outs = [acc[r * tq:(r + 1) * tq, :LANES] / (acc[r * tq:(r + 1) * tq, LANES:LANES + 1] + sink_p[r])
                for r in range(n_rep)]
        for pair in range(2):
            gq = 2 * hk + pair
            o_ref[:, gq * LANES:(gq + 1) * LANES] = jnp.where(
                low_half, outs[2 * pair], outs[2 * pair + 1]).astype(BF16)


def _swa_online_kernel(sink_ref, q_ref, k_ref, v_ref, km_ref, vm_ref, o_ref, *, tq, win, seq):
    start, ok = _swa_window(tq, win, seq)
    low_half = lax.broadcasted_iota(jnp.int32, (tq, LANES), 1) < SWA_DH
    for hk in range(SWA_HKV):
        cols = slice(hk * LANES, (hk + 1) * LANES)
        kw = k_ref[pl.ds(start, win), cols]
        vw = v_ref[pl.ds(start, win), cols]
        km = km_ref[:, cols]
        vm = vm_ref[:, cols]
        for pair in range(2):
            gq = 2 * hk + pair
            qp = q_ref[:, gq * LANES:(gq + 1) * LANES]
            outs = []
            for half in range(2):
                sink = sink_ref[2 * gq + half]
                qm = jnp.where(low_half == (half == 0), qp, jnp.zeros_like(qp))
                s = jnp.where(ok, _nt_dot(qm, kw), NEG_INF)
                sm = _nt_dot(qm, km)
                m = jnp.maximum(jnp.maximum(jnp.max(s, axis=-1, keepdims=True),
                                            jnp.max(sm, axis=-1, keepdims=True)), sink)
                p = jnp.exp2(s - m)
                pm = jnp.exp2(sm - m)
                l = (jnp.sum(p, axis=-1, keepdims=True) + jnp.sum(pm, axis=-1, keepdims=True)
                     + jnp.exp2(sink - m))
                o = (jnp.dot(p.astype(BF16), vw, preferred_element_type=F32)
                     + jnp.dot(pm.astype(BF16), vm, preferred_element_type=F32))
                outs.append(o / l)
            o_ref[:, gq * LANES:(gq + 1) * LANES] = jnp.where(low_half, outs[0], outs[1]).astype(BF16)


def _swa(proj, proj_meta, sink2, tq, win, bounded):
    b, seq, _ = proj.shape
    kern = functools.partial(_swa_bounded_kernel if bounded else _swa_online_kernel, tq=tq, win=win, seq=seq)
    return pl.pallas_call(
        kern,
        grid=(b, seq // tq),
        in_specs=[
            pl.BlockSpec(memory_space=pltpu.SMEM),
            pl.BlockSpec((None, tq, 4 * LANES), lambda bi, i: (bi, i, G_QA // 4)),
            pl.BlockSpec((None, seq, 2 * LANES), lambda bi, i: (bi, 0, G_KA // 2)),
            pl.BlockSpec((None, seq, 2 * LANES), lambda bi, i: (bi, 0, G_VA // 2)),
            pl.BlockSpec((N_META, 2 * LANES), lambda bi, i: (0, G_KA // 2)),
            pl.BlockSpec((N_META, 2 * LANES), lambda bi, i: (0, G_VA // 2)),
        ],
        out_specs=pl.BlockSpec((None, tq, 4 * LANES), lambda bi, i: (bi, i, 0)),
        out_shape=jax.ShapeDtypeStruct((b, seq, 4 * LANES), BF16),
        compiler_params=_cparams(("parallel", "parallel")),
        name="swa_bounded" if bounded else "swa_online",
    )(sink2, proj, proj, proj, proj_meta, proj_meta)


def _quarter_masked(q):
    lane = lax.broadcasted_iota(jnp.int32, q.shape, 1)
    return [jnp.where((lane >= DIFF_DQ * c) & (lane < DIFF_DQ * (c + 1)), q, jnp.zeros_like(q))
            for c in range(4)]


def _with_ones(v):
    return jnp.concatenate([v, jnp.ones_like(v)], axis=1)


def _diff_finish(accs, lam, gh, o_ref):
    def normalized(a):
        return a[:, :LANES] / a[:, LANES:LANES + 1]

    low = lax.broadcasted_iota(jnp.int32, (accs[0].shape[0], LANES), 1) < DIFF_DV
    o = jnp.where(low, normalized(accs[0]) - lam * normalized(accs[1]),
                  normalized(accs[2]) - lam * normalized(accs[3]))
    o2 = o * o
    ss_lo = jnp.sum(jnp.where(low, o2, 0.0), axis=-1, keepdims=True)
    ss_hi = jnp.sum(jnp.where(low, 0.0, o2), axis=-1, keepdims=True)
    rs = lax.rsqrt(jnp.where(low, ss_lo, ss_hi) * (1.0 / DIFF_DV) + 1e-5)
    o_ref[...] = (o * rs * gh).astype(BF16)


def _diff_bounded_kernel(sc_ref, q_ref, k_ref, v_ref, km_ref, vm_ref, gh_ref, o_ref, *, tq, tk, seq):
    bound = sc_ref[1]
    q4 = jnp.concatenate(_quarter_masked(q_ref[...]), axis=0)
    pm = jnp.exp2(_nt_dot(q4, km_ref[...]) - bound).astype(BF16)
    acc = jnp.dot(pm, _with_ones(vm_ref[...]), preferred_element_type=F32)
    for j in range(seq // tk):
        p = jnp.exp2(_nt_dot(q4, k_ref[j * tk:(j + 1) * tk, :]) - bound).astype(BF16)
        acc = acc + jnp.dot(p, _with_ones(v_ref[j * tk:(j + 1) * tk, :]), preferred_element_type=F32)
    _diff_finish([acc[c * tq:(c + 1) * tq] for c in range(4)], sc_ref[0], gh_ref[...], o_ref)


def _diff_online_kernel(sc_ref, q_ref, k_ref, v_ref, km_ref, vm_ref, gh_ref, o_ref, m_ref, acc_ref,
                        *, tq, tk, seq):
    qms = _quarter_masked(q_ref[...])
    km = km_ref[...]
    vm = _with_ones(vm_ref[...])
    for c in range(4):
        s = _nt_dot(qms[c], km)
        m = jnp.max(s, axis=-1, keepdims=True)
        p = jnp.exp2(s - m)
        m_ref[c] = m
        acc_ref[c] = jnp.dot(p.astype(BF16), vm, preferred_element_type=F32)

    def body(j, carry):
        off = pl.multiple_of(j * tk, tk)
        kt = k_ref[pl.ds(off, tk), :]
        vt = _with_ones(v_ref[pl.ds(off, tk), :])
        for c in range(4):
            s = _nt_dot(qms[c], kt)
            m_old = m_ref[c]
            m_new = jnp.maximum(m_old, jnp.max(s, axis=-1, keepdims=True))
            p = jnp.exp2(s - m_new)
            acc_ref[c] = (jnp.exp2(m_old - m_new) * acc_ref[c]
                          + jnp.dot(p.astype(BF16), vt, preferred_element_type=F32))
            m_ref[c] = m_new
        return carry

    lax.fori_loop(0, seq // tk, body, 0)
    _diff_finish([acc_ref[c] for c in range(4)], sc_ref[0], gh_ref[...], o_ref)


def _diff(proj, proj_meta, scalars, gh, tq, tk, bounded):
    b, seq, _ = proj.shape
    n_pairs = DIFF_H // 2
    if bounded:
        kern = functools.partial(_diff_bounded_kernel, tq=tq, tk=tk, seq=seq)
        scratch = []
    else:
        kern = functools.partial(_diff_online_kernel, tq=tq, tk=tk, seq=seq)
        scratch = [pltpu.VMEM((4, tq, 1), F32), pltpu.VMEM((4, tq, 2 * LANES), F32)]
    return pl.pallas_call(
        kern,
        grid=(b, n_pairs, seq // tq),
        in_specs=[
            pl.BlockSpec(memory_space=pltpu.SMEM),
            pl.BlockSpec((None, tq, LANES), lambda bi, hp, i: (bi, i, G_QB + hp)),
            pl.BlockSpec((None, seq, LANES), lambda bi, hp, i: (bi, 0, G_KB + hp)),
            pl.BlockSpec((None, seq, LANES), lambda bi, hp, i: (bi, 0, G_VB + hp)),
            pl.BlockSpec((N_META, LANES), lambda bi, hp, i: (0, G_KB + hp)),
            pl.BlockSpec((N_META, LANES), lambda bi, hp, i: (0, G_VB + hp)),
            pl.BlockSpec((1, LANES), lambda bi, hp, i: (0, 0)),
        ],
        out_specs=pl.BlockSpec((None, tq, LANES), lambda bi, hp, i: (bi, i, hp)),
        out_shape=jax.ShapeDtypeStruct((b, seq, n_pairs * LANES), BF16),
        scratch_shapes=scratch,
        compiler_params=_cparams(("parallel", "parallel", "parallel")),
        name="diff_bounded" if bounded else "diff_online",
    )(scalars, proj, proj, proj, proj_meta, proj_meta, gh)


def _outproj_kernel(ma_ref, mb_ref, x_ref, wo_ref, g_ref, wrh_ref, wrl_ref, br_ref, tri_ref,
                    h1_ref, v_ref, idx_ref, gate_ref, rank_ref, cnt_ref, carry_ref, *, tm):
    half = wo_ref.shape[0] // 2

    @pl.when(pl.program_id(0) == 0)
    def _():
        carry_ref[...] = jnp.zeros_like(carry_ref)

    h1 = (x_ref[...]
          + jnp.dot(ma_ref[...], wo_ref[:half, :], preferred_element_type=F32)
          + jnp.dot(mb_ref[...], wo_ref[half:, :], preferred_element_type=F32))
    h1_ref[...] = h1
    v = h1 * lax.rsqrt(jnp.mean(h1 * h1, axis=-1, keepdims=True) + EPS) * g_ref[...]
    v_ref[...] = v
    vh = v.astype(BF16)
    vl = (v - vh.astype(F32)).astype(BF16)
    work = (_nt_dot(wrh_ref[...], vh) + _nt_dot(wrl_ref[...], vh) + _nt_dot(wrh_ref[...], vl)
            + br_ref[...])
    iota_e = lax.broadcasted_iota(jnp.int32, (N_EXPERTS, tm), 0)
    vals, idxs, sels = [], [], []
    for _ in range(TOP_K):
        mk = jnp.max(work, axis=0, keepdims=True)
        ik = jnp.min(jnp.where(work == mk, iota_e, N_EXPERTS), axis=0, keepdims=True)
        sel = iota_e == ik
        work = jnp.where(sel, -jnp.inf, work)
        vals.append(mk)
        idxs.append(ik)
        sels.append(sel)
    exps = [jnp.exp(vk - vals[0]) for vk in vals]
    denom = exps[0] + exps[1] + exps[2] + exps[3]
    gate_ref[...] = jnp.concatenate([e / denom for e in exps], axis=0)
    idx_ref[...] = jnp.concatenate(idxs, axis=0)
    cnt = jnp.zeros((N_EXPERTS, tm), F32)
    for sel in sels:
        cnt = cnt + jnp.where(sel, 1.0, 0.0)
    before = jnp.dot(cnt.astype(BF16), tri_ref[...], preferred_element_type=F32) + carry_ref[...]
    ranks = [jnp.sum(jnp.where(sel, before, 0.0), axis=0, keepdims=True) for sel in sels]
    rank_ref[...] = jnp.concatenate(ranks, axis=0).astype(jnp.int32)
    carry_ref[...] = carry_ref[...] + jnp.sum(cnt, axis=1, keepdims=True)
    cnt_ref[...] = jnp.broadcast_to(carry_ref[...], cnt_ref.shape)


def _outproj(mixed_a, mixed_b, x2, wo, g, wrh, wrl, br, tri, tm):
    t, d = x2.shape
    hw = mixed_a.shape[1]
    kern = functools.partial(_outproj_kernel, tm=tm)
    row = lambda i: (i, 0)
    fix = lambda i: (0, 0)
    col = lambda i: (0, i)
    return pl.pallas_call(
        kern,
        grid=(t // tm,),
        in_specs=[
            pl.BlockSpec((tm, hw), row), pl.BlockSpec((tm, hw), row), pl.BlockSpec((tm, d), row),
            pl.BlockSpec((d, d), fix), pl.BlockSpec((1, d), fix),
            pl.BlockSpec((N_EXPERTS, d), fix), pl.BlockSpec((N_EXPERTS, d), fix),
            pl.BlockSpec((N_EXPERTS, 1), fix), pl.BlockSpec((tm, tm), fix),
        ],
        out_specs=[
            pl.BlockSpec((tm, d), row), pl.BlockSpec((tm, d), row),
            pl.BlockSpec((TOP_K, tm), col), pl.BlockSpec((TOP_K, tm), col),
            pl.BlockSpec((TOP_K, tm), col), pl.BlockSpec((N_EXPERTS, LANES), fix),
        ],
        out_shape=[
            jax.ShapeDtypeStruct((t, d), F32), jax.ShapeDtypeStruct((t, d), F32),
            jax.ShapeDtypeStruct((TOP_K, t), jnp.int32), jax.ShapeDtypeStruct((TOP_K, t), F32),
            jax.ShapeDtypeStruct((TOP_K, t), jnp.int32),
            jax.ShapeDtypeStruct((N_EXPERTS, LANES), F32),
        ],
        scratch_shapes=[pltpu.VMEM((N_EXPERTS, 1), F32)],
        compiler_params=_cparams(("arbitrary",)), name="outproj",
    )(mixed_a, mixed_b, x2, wo, g, wrh, wrl, br, tri)


def _dispatch_kernel(fill_ref, dest_ref, v_ref, xs_hbm, zero_ref, sem, zsem, *, tmd, tme):
    n_tiles = xs_hbm.shape[0] // tme

    @pl.when(pl.program_id(0) == 0)
    def _():
        zero_ref[...] = jnp.zeros_like(zero_ref)

        def row_copy(r):
            return pltpu.make_async_copy(zero_ref.at[pl.ds(0, 1)], xs_hbm.at[pl.ds(r, 1)], zsem)

        def tile_copy(i):
            return pltpu.make_async_copy(zero_ref, xs_hbm.at[pl.ds(i * tme, tme)], zsem)

        def fill(start, wait):
            for e in range(N_EXPERTS):
                def one_row(r, carry):
                    row_copy(r).start() if start else row_copy(r).wait()
                    return carry
                lax.fori_loop(fill_ref[e], fill_ref[N_EXPERTS + e], one_row, 0)

            def one_tile(i, carry):
                tile_copy(i).start() if start else tile_copy(i).wait()
                return carry
            lax.fori_loop(fill_ref[2 * N_EXPERTS], n_tiles, one_tile, 0)

        fill(True, False)
        fill(False, True)

    def issue(t, carry):
        for k in range(TOP_K):
            pltpu.make_async_copy(v_ref.at[pl.ds(t, 1)],
                                  xs_hbm.at[pl.ds(dest_ref[TOP_K * t + k], 1)], sem).start()
        return carry

    lax.fori_loop(0, tmd, issue, 0, unroll=4)
    for k in range(TOP_K):
        pltpu.make_async_copy(v_ref, xs_hbm.at[pl.ds(0, tmd)], sem).wait()


def _dispatch(fill_meta, dest_flat, v, n_slots, tmd, tme):
    t, d = v.shape
    kern = functools.partial(_dispatch_kernel, tmd=tmd, tme=tme)
    grid_spec = pltpu.PrefetchScalarGridSpec(
        num_scalar_prefetch=1,
        grid=(t // tmd,),
        in_specs=[
            pl.BlockSpec((TOP_K * tmd,), lambda i, fm: (i,), memory_space=pltpu.SMEM),
            pl.BlockSpec((tmd, d), lambda i, fm: (i, 0)),
        ],
        out_specs=pl.BlockSpec(memory_space=pl.ANY),
        scratch_shapes=[pltpu.VMEM((tme, d), F32), pltpu.SemaphoreType.DMA, pltpu.SemaphoreType.DMA],
    )
    return pl.pallas_call(
        kern,
        grid_spec=grid_spec,
        out_shape=jax.ShapeDtypeStruct((n_slots, d), F32),
        compiler_params=_cparams(("arbitrary",)), name="dispatch",
    )(fill_meta, dest_flat, v)


def _experts_kernel(te_ref, ts_ref, tv_ref, xs_ref, w1_ref, b1_ref, w2_ref, b2_ref, ys_ref):
    del te_ref, ts_ref
    d_ff = w2_ref.shape[0]

    @pl.when(tv_ref[pl.program_id(0)] > 0)
    def _():
        a = jnp.dot(xs_ref[...].astype(BF16), w1_ref[...], preferred_element_type=F32) + b1_ref[...]
        glu = jnp.minimum(a[:, :d_ff], SWIGLU_LIMIT)
        lin = jnp.clip(a[:, d_ff:], -SWIGLU_LIMIT, SWIGLU_LIMIT)
        act = glu * jax.nn.sigmoid(SWIGLU_ALPHA * glu) * (lin + 1.0)
        ys_ref[...] = jnp.dot(act.astype(BF16), w2_ref[...], preferred_element_type=F32) + b2_ref[...]

    @pl.when(tv_ref[pl.program_id(0)] == 0)
    def _():
        ys_ref[...] = jnp.zeros_like(ys_ref)


def _experts(tile_expert, tile_src, tile_valid, xs, w1, b1, w2, b2, tm):
    n_slots, d = xs.shape
    two_ff = w1.shape[2]
    d_ff = two_ff // 2
    grid_spec = pltpu.PrefetchScalarGridSpec(
        num_scalar_prefetch=3,
        grid=(n_slots // tm,),
        in_specs=[
            pl.BlockSpec((tm, d), lambda i, te, ts, tv: (ts[i], 0)),
            pl.BlockSpec((None, d, two_ff), lambda i, te, ts, tv: (te[i], 0, 0)),
            pl.BlockSpec((None, 1, two_ff), lambda i, te, ts, tv: (te[i], 0, 0)),
            pl.BlockSpec((None, d_ff, d), lambda i, te, ts, tv: (te[i], 0, 0)),
            pl.BlockSpec((None, 1, d), lambda i, te, ts, tv: (te[i], 0, 0)),
        ],
        out_specs=pl.BlockSpec((tm, d), lambda i, te, ts, tv: (i, 0)),
    )
    return pl.pallas_call(
        _experts_kernel,
        grid_spec=grid_spec,
        out_shape=jax.ShapeDtypeStruct(xs.shape, F32),
        compiler_params=_cparams(("arbitrary",)), name="experts",
    )(tile_expert, tile_src, tile_valid, xs, w1, b1, w2, b2)


def _combine_kernel(dest_ref, dest_next_ref, gate_ref, h1_ref, ys_hbm, o_ref, buf_ref, sem, *, tmc):
    i = pl.program_id(0)
    n = pl.num_programs(0)

    def issue(d_ref, slot):
        def one(t, carry):
            for k in range(TOP_K):
                pltpu.make_async_copy(ys_hbm.at[pl.ds(d_ref[TOP_K * t + k], 1)],
                                      buf_ref.at[slot, k, pl.ds(t, 1)], sem.at[slot]).start()
            return carry
        lax.fori_loop(0, tmc, one, 0, unroll=4)

    @pl.when(i == 0)
    def _():
        issue(dest_ref, 0)

    slot = i % 2

    @pl.when(i + 1 < n)
    def _():
        issue(dest_next_ref, 1 - slot)

    pltpu.make_async_copy(buf_ref.at[slot], buf_ref.at[slot], sem.at[slot]).wait()
    acc = h1_ref[...]
    for k in range(TOP_K):
        acc = acc + gate_ref[:, k:k + 1] * buf_ref[slot, k]
    o_ref[...] = acc


def _combine(dest_flat, gate_tk, h1, ys, tmc):
    t, d = h1.shape
    n = t // tmc
    kern = functools.partial(_combine_kernel, tmc=tmc)
    return pl.pallas_call(
        kern,
        grid=(n,),
        in_specs=[
            pl.BlockSpec((TOP_K * tmc,), lambda i: (i,), memory_space=pltpu.SMEM),
            pl.BlockSpec((TOP_K * tmc,), lambda i: (jnp.minimum(i + 1, n - 1),), memory_space=pltpu.SMEM),
            pl.BlockSpec((tmc, TOP_K), lambda i: (i, 0)),
            pl.BlockSpec((tmc, d), lambda i: (i, 0)),
            pl.BlockSpec(memory_space=pl.ANY),
        ],
        out_specs=pl.BlockSpec((tmc, d), lambda i: (i, 0)),
        out_shape=jax.ShapeDtypeStruct((t, d), F32),
        scratch_shapes=[
            pltpu.VMEM((2, TOP_K, tmc, d), F32),
            pltpu.SemaphoreType.DMA((2,)),
        ],
        compiler_params=_cparams(("arbitrary",)), name="combine",
    )(dest_flat, dest_flat, gate_tk, h1, ys)


def _swap_halves(g, dh):
    return jnp.concatenate([g[dh // 2:], g[:dh // 2]])


def _rope_tables(pos, dh, gq, gk, q_scale):
    inv = ROPE_THETA ** (-jnp.arange(0, dh, 2, dtype=F32) / dh)
    ang = pos.astype(F32)[:, None] * inv[None, :]
    cos = jnp.concatenate([jnp.cos(ang), jnp.cos(ang)], axis=1)
    sin = jnp.concatenate([-jnp.sin(ang), jnp.sin(ang)], axis=1)
    rep = LANES // dh
    tile = lambda a: jnp.tile(a, (1, rep))
    gq, gk = gq.astype(F32), gk.astype(F32)
    return jnp.stack([
        tile(cos * gq[None]) * q_scale, tile(sin * _swap_halves(gq, dh)[None]) * q_scale,
        tile(cos * gk[None]), tile(sin * _swap_halves(gk, dh)[None]),
    ])


def _rope_matrix(dh):
    j = np.arange(LANES)
    swap = (j // dh) * dh + (j % dh + dh // 2) % dh
    m = np.zeros((2 * LANES, 2 * LANES), np.float32)
    m[swap, j] = 1.0
    m[LANES:, LANES:] = (j[:, None] // dh == j[None, :] // dh)
    return jnp.asarray(m, BF16)


def _pick_tile(n, pref):
    t = min(n, pref)
    while n % t:
        t //= 2
    return t


def kernel(x, meta_tokens, g_attn, w_in, g_q_swa, g_k_swa, g_q_diff, g_k_diff, sink_swa, lambda_q1, lambda_k1, lambda_q2, lambda_k2, g_diff_head, w_out, g_ffn, w_router, b_router, w_mlp1, b_mlp1, w_mlp2, b_mlp2):
    b, seq, d = x.shape
    assert g_attn.shape[0] == 1 and d % LANES == 0 and seq % 512 == 0
    t = b * seq
    lambda_init = 0.8 - 0.6 * math.exp(-0.3 * 0)

    wi = w_in[0]
    q_a, k_a, v_a, q_b, k_b, v_b = (wi[:, s:e] for s, e in
                                    ((0, 512), (512, 640), (640, 768), (768, 1280), (1280, 1792), (1792, 2304)))
    dup = lambda w: jnp.concatenate([w[:, :64], w[:, :64], w[:, 64:], w[:, 64:]], axis=1)
    w_wide = jnp.concatenate([q_a, dup(k_a), dup(v_a), q_b, k_b, v_b], axis=1).astype(BF16)
    mat_a, mat_b = _rope_matrix(SWA_DH), _rope_matrix(DIFF_DQ)

    def tables(pos):
        return jnp.concatenate([
            _rope_tables(pos, SWA_DH, g_q_swa[0], g_k_swa[0], SWA_DH ** -0.5 * LOG2E),
            _rope_tables(pos, DIFF_DQ, g_q_diff[0], g_k_diff[0], DIFF_DQ ** -0.5 * LOG2E)])

    tab_tok = tables(jnp.arange(N_META, N_META + seq))
    tab_meta = tables(jnp.arange(N_META))
    g_attn2 = g_attn[0].reshape(1, d).astype(F32)

    x2 = x.reshape(t, d)
    tm = _pick_tile(seq, 512)
    proj = _inproj(x2, g_attn2, w_wide, tab_tok, mat_a, mat_b, tm).reshape(b, seq, PROJ_W)
    proj_meta = _inproj(meta_tokens.astype(F32), g_attn2, w_wide, tab_meta, mat_a, mat_b, N_META)

    def score_bound(dh, gq, gk):
        return dh ** 0.5 * LOG2E * jnp.max(jnp.abs(gq.astype(F32))) * jnp.max(jnp.abs(gk.astype(F32)))

    bound_a = score_bound(SWA_DH, g_q_swa[0], g_k_swa[0])
    swa_scalars = jnp.concatenate([sink_swa[0].astype(F32) * LOG2E, bound_a[None]])
    mixed_a = lax.cond(
        bound_a <= MAX_SHIFT_BOUND,
        lambda: _swa(proj, proj_meta, swa_scalars, 256, 512, True),
        lambda: _swa(proj, proj_meta, swa_scalars, 256, 512, False))
    lam = (jnp.exp(jnp.sum(lambda_q1[0].astype(F32) * lambda_k1[0].astype(F32)))
           - jnp.exp(jnp.sum(lambda_q2[0].astype(F32) * lambda_k2[0].astype(F32))) + lambda_init)
    gh = (jnp.tile(g_diff_head[0].astype(F32), 2) * (1.0 - lambda_init)).reshape(1, LANES)
    bound_b = score_bound(DIFF_DQ, g_q_diff[0], g_k_diff[0])
    scalars = jnp.stack([lam, bound_b])
    mixed_b = lax.cond(
        bound_b <= MAX_SHIFT_BOUND,
        lambda: _diff(proj, proj_meta, scalars, gh, _pick_tile(seq, 256), _pick_tile(seq, 1024), True),
        lambda: _diff(proj, proj_meta, scalars, gh, _pick_tile(seq, 512), _pick_tile(seq, 512), False))

    wr = w_router[0].T.astype(F32)
    wrh = wr.astype(BF16)
    wrl = (wr - wrh.astype(F32)).astype(BF16)
    tri = jnp.asarray(np.triu(np.ones((tm, tm), np.float32), 1), BF16)
    h1, v, idx_t, gate_t, rank_t, cnt = _outproj(
        mixed_a.reshape(t, -1), mixed_b.reshape(t, -1), x2, w_out[0].astype(BF16),
        g_ffn[0].reshape(1, d).astype(F32), wrh, wrl, b_router[0].reshape(N_EXPERTS, 1).astype(F32),
        tri, tm)

    tme = 512
    counts = cnt[:, 0].astype(jnp.int32)
    padded = (counts + tme - 1) // tme * tme
    pad_end = jnp.cumsum(padded)
    pad_start = pad_end - padded
    start_of = jnp.sum(jnp.where(idx_t[None] == jnp.arange(N_EXPERTS)[:, None, None],
                                 pad_start[:, None, None], 0), axis=0)
    dest_flat = (start_of + rank_t).T.reshape(-1)
    gate_tk = gate_t.T
    n_tiles = -(-(TOP_K * t) // tme) + N_EXPERTS
    n_slots = n_tiles * tme
    tiles = jnp.arange(n_tiles, dtype=jnp.int32)
    n_valid = pad_end[-1] // tme
    tile_src = jnp.minimum(tiles, n_valid - 1)
    tile_expert = jnp.minimum(jnp.sum(pad_end[None, :] <= (tile_src * tme)[:, None], axis=1),
                              N_EXPERTS - 1).astype(jnp.int32)
    tile_valid = (tiles < n_valid).astype(jnp.int32)

    fill_meta = jnp.concatenate([pad_start + counts, pad_end, n_valid[None]]).astype(jnp.int32)
    xs = _dispatch(fill_meta, dest_flat, v, n_slots, _pick_tile(t, 512), tme)
    ys = _experts(tile_expert, tile_src, tile_valid, xs,
                  w_mlp1[0].astype(BF16), b_mlp1[0].reshape(N_EXPERTS, 1, -1).astype(F32),
                  w_mlp2[0].astype(BF16), b_mlp2[0].reshape(N_EXPERTS, 1, -1).astype(F32), tme)
    out = _combine(dest_flat, gate_tk, h1, ys, _pick_tile(t, 256))
    return out.reshape(b, seq, d)
```

```python
import functools
import math

import numpy as np
import jax
import jax.numpy as jnp
from jax import lax
from jax.experimental import pallas as pl
from jax.experimental.pallas import tpu as pltpu

F32 = jnp.float32
BF16 = jnp.bfloat16

N_META = 16
WINDOW = 128
ROPE_THETA = 10000.0
EPS = 1e-6
NEG_INF = -1e30
SWA_HQ, SWA_HKV, SWA_DH = 8, 2, 64
DIFF_H, DIFF_DQ, DIFF_DV = 8, 32, 64
N_EXPERTS = 32
TOP_K = 4
SWIGLU_LIMIT = 7.0
SWIGLU_ALPHA = 1.702
LOG2E = math.log2(math.e)
LANES = 128
VMEM_LIMIT = 56 * 1024 * 1024
MAX_SHIFT_BOUND = 60.0

G_QA, G_KA, G_VA, G_QB, G_KB, G_VB = 0, 4, 6, 8, 12, 16
N_GROUPS = 20
PROJ_W = N_GROUPS * LANES


def _nt_dot(a, b):
    return lax.dot_general(a, b, (((1,), (1,)), ((), ())), preferred_element_type=F32)


def _cparams(sem):
    return pltpu.CompilerParams(dimension_semantics=sem, vmem_limit_bytes=VMEM_LIMIT)


def _inproj_kernel(x_ref, g_ref, w_ref, tab_ref, ma_ref, mb_ref, o_ref):
    x = x_ref[...]
    ms = jnp.mean(x * x, axis=-1, keepdims=True)
    u = (x * lax.rsqrt(ms + EPS) * g_ref[...]).astype(BF16)

    def rope_group(y, mat, inv_dh, c, s):
        z = jnp.concatenate([y.astype(BF16), (y * y).astype(BF16)], axis=1)
        r = jnp.dot(z, mat, preferred_element_type=F32)
        sw, ss = r[:, :LANES], r[:, LANES:]
        return lax.rsqrt(ss * inv_dh + EPS) * (y * c + sw * s)

    chunk = 4 * LANES
    for c0 in range(0, N_GROUPS, 4):
        y4 = jnp.dot(u, w_ref[:, c0 * LANES:c0 * LANES + chunk], preferred_element_type=F32)
        for j in range(4):
            grp = c0 + j
            y = y4[:, j * LANES:(j + 1) * LANES]
            if G_QA <= grp < G_KA:
                y = rope_group(y, ma_ref[...], 1.0 / SWA_DH, tab_ref[0], tab_ref[1])
            elif G_KA <= grp < G_VA:
                y = rope_group(y, ma_ref[...], 1.0 / SWA_DH, tab_ref[2], tab_ref[3])
            elif G_QB <= grp < G_KB:
                y = rope_group(y, mb_ref[...], 1.0 / DIFF_DQ, tab_ref[4], tab_ref[5])
            elif G_KB <= grp < G_VB:
                y = rope_group(y, mb_ref[...], 1.0 / DIFF_DQ, tab_ref[6], tab_ref[7])
            o_ref[:, grp * LANES:(grp + 1) * LANES] = y.astype(BF16)


def _inproj(x2, g, w, tab, mat_a, mat_b, tm):
    rows, d = x2.shape
    n_tab = tab.shape[1] // tm
    return pl.pallas_call(
        _inproj_kernel,
        grid=(rows // tm,),
        in_specs=[
            pl.BlockSpec((tm, d), lambda i: (i, 0)),
            pl.BlockSpec((1, d), lambda i: (0, 0)),
            pl.BlockSpec((d, PROJ_W), lambda i: (0, 0)),
            pl.BlockSpec((8, tm, LANES), lambda i: (0, i % n_tab, 0)),
            pl.BlockSpec((2 * LANES, 2 * LANES), lambda i: (0, 0)),
            pl.BlockSpec((2 * LANES, 2 * LANES), lambda i: (0, 0)),
        ],
        out_specs=pl.BlockSpec((tm, PROJ_W), lambda i: (i, 0)),
        out_shape=jax.ShapeDtypeStruct((rows, PROJ_W), BF16),
        compiler_params=_cparams(("parallel",)), name="inproj",
    )(x2, g, w, tab, mat_a, mat_b)


def _swa_window(tq, win, seq):
    t0 = pl.program_id(1) * tq
    start = pl.multiple_of(jnp.clip(t0 - WINDOW, 0, seq - win), LANES)
    qpos = t0 + lax.broadcasted_iota(jnp.int32, (tq, win), 0)
    kpos = start + lax.broadcasted_iota(jnp.int32, (tq, win), 1)
    return start, jnp.abs(qpos - kpos) <= WINDOW


def _swa_bounded_kernel(sc_ref, q_ref, k_ref, v_ref, km_ref, vm_ref, o_ref, *, tq, win, seq):
    start, ok = _swa_window(tq, win, seq)
    bound = sc_ref[SWA_HQ]
    low_half = lax.broadcasted_iota(jnp.int32, (tq, LANES), 1) < SWA_DH
    n_rep = SWA_HQ // SWA_HKV
    for hk in range(SWA_HKV):
        cols = slice(hk * LANES, (hk + 1) * LANES)
        qs = []
        for pair in range(2):
            qp = q_ref[:, (2 * hk + pair) * LANES:(2 * hk + pair + 1) * LANES]
            qs += [jnp.where(low_half, qp, jnp.zeros_like(qp)), jnp.where(low_half, jnp.zeros_like(qp), qp)]
        q4 = jnp.concatenate(qs, axis=0)
        s = _nt_dot(q4, k_ref[pl.ds(start, win), cols])
        sm = _nt_dot(q4, km_ref[:, cols])
        ps, pms, sink_p = [], [], []
        for r in range(n_rep):
            sink = sc_ref[n_rep * hk + r]
            shift = jnp.maximum(bound, sink)
            rows = slice(r * tq, (r + 1) * tq)
            ps.append(jnp.exp2(jnp.where(ok, s[rows] - shift, NEG_INF)).astype(BF16))
            pms.append(jnp.exp2(sm[rows] - shift).astype(BF16))
            sink_p.append(jnp.exp2(jnp.full((tq, 1), sink - shift, F32)))
        acc = (jnp.dot(jnp.concatenate(ps, axis=0), _with_ones(v_ref[pl.ds(start, win), cols]),
                       preferred_element_type=F32)
               + jnp.dot(jnp.concatenate(pms, axis=0), _with_ones(vm_ref[:, cols]),
                         preferred_element_type=F32))
        outs = [acc[r * tq:(r + 1) * tq, :LANES] / (acc[r * tq:(r + 1) * tq, LANES:LANES + 1] + sink_p[r])
                for r in range(n_rep)]
        for pair in range(2):
            gq = 2 * hk + pair
            o_ref[:, gq * LANES:(gq + 1) * LANES] = jnp.where(
                low_half, outs[2 * pair], outs[2 * pair + 1]).astype(BF16)


def _swa_online_kernel(sink_ref, q_ref, k_ref, v_ref, km_ref, vm_ref, o_ref, *, tq, win, seq):
    start, ok = _swa_window(tq, win, seq)
    low_half = lax.broadcasted_iota(jnp.int32, (tq, LANES), 1) < SWA_DH
    for hk in range(SWA_HKV):
        cols = slice(hk * LANES, (hk + 1) * LANES)
        kw = k_ref[pl.ds(start, win), cols]
        vw = v_ref[pl.ds(start, win), cols]
        km = km_ref[:, cols]
        vm = vm_ref[:, cols]
        for pair in range(2):
            gq = 2 * hk + pair
            qp = q_ref[:, gq * LANES:(gq + 1) * LANES]
            outs = []
            for half in range(2):
                sink = sink_ref[2 * gq + half]
                qm = jnp.where(low_half == (half == 0), qp, jnp.zeros_like(qp))
                s = jnp.where(ok, _nt_dot(qm, kw), NEG_INF)
                sm = _nt_dot(qm, km)
                m = jnp.maximum(jnp.maximum(jnp.max(s, axis=-1, keepdims=True),
                                            jnp.max(sm, axis=-1, keepdims=True)), sink)
                p = jnp.exp2(s - m)
                pm = jnp.exp2(sm - m)
                l = (jnp.sum(p, axis=-1, keepdims=True) + jnp.sum(pm, axis=-1, keepdims=True)
                     + jnp.exp2(sink - m))
                o = (jnp.dot(p.astype(BF16), vw, preferred_element_type=F32)
                     + jnp.dot(pm.astype(BF16), vm, preferred_element_type=F32))
                outs.append(o / l)
            o_ref[:, gq * LANES:(gq + 1) * LANES] = jnp.where(low_half, outs[0], outs[1]).astype(BF16)


def _swa(proj, proj_meta, sink2, tq, win, bounded):
    b, seq, _ = proj.shape
    kern = functools.partial(_swa_bounded_kernel if bounded else _swa_online_kernel, tq=tq, win=win, seq=seq)
    return pl.pallas_call(
        kern,
        grid=(b, seq // tq),
        in_specs=[
            pl.BlockSpec(memory_space=pltpu.SMEM),
            pl.BlockSpec((None, tq, 4 * LANES), lambda bi, i: (bi, i, G_QA // 4)),
            pl.BlockSpec((None, seq, 2 * LANES), lambda bi, i: (bi, 0, G_KA // 2)),
            pl.BlockSpec((None, seq, 2 * LANES), lambda bi, i: (bi, 0, G_VA // 2)),
            pl.BlockSpec((N_META, 2 * LANES), lambda bi, i: (0, G_KA // 2)),
            pl.BlockSpec((N_META, 2 * LANES), lambda bi, i: (0, G_VA // 2)),
        ],
        out_specs=pl.BlockSpec((None, tq, 4 * LANES), lambda bi, i: (bi, i, 0)),
        out_shape=jax.ShapeDtypeStruct((b, seq, 4 * LANES), BF16),
        compiler_params=_cparams(("parallel", "parallel")),
        name="swa_bounded" if bounded else "swa_online",
    )(sink2, proj, proj, proj, proj_meta, proj_meta)


def _quarter_masked(q):
    lane = lax.broadcasted_iota(jnp.int32, q.shape, 1)
    return [jnp.where((lane >= DIFF_DQ * c) & (lane < DIFF_DQ * (c + 1)), q, jnp.zeros_like(q))
            for c in range(4)]


def _with_ones(v):
    return jnp.concatenate([v, jnp.ones_like(v)], axis=1)


def _diff_finish(accs, lam, gh, o_ref):
    def normalized(a):
        return a[:, :LANES] / a[:, LANES:LANES + 1]

    low = lax.broadcasted_iota(jnp.int32, (accs[0].shape[0], LANES), 1) < DIFF_DV
    o = jnp.where(low, normalized(accs[0]) - lam * normalized(accs[1]),
                  normalized(accs[2]) - lam * normalized(accs[3]))
    o2 = o * o
    ss_lo = jnp.sum(jnp.where(low, o2, 0.0), axis=-1, keepdims=True)
    ss_hi = jnp.sum(jnp.where(low, 0.0, o2), axis=-1, keepdims=True)
    rs = lax.rsqrt(jnp.where(low, ss_lo, ss_hi) * (1.0 / DIFF_DV) + 1e-5)
    o_ref[...] = (o * rs * gh).astype(BF16)


def _diff_bounded_kernel(sc_ref, q_ref, k_ref, v_ref, km_ref, vm_ref, gh_ref, o_ref, *, tq, tk, seq):
    bound = sc_ref[1]
    q4 = jnp.concatenate(_quarter_masked(q_ref[...]), axis=0)
    pm = jnp.exp2(_nt_dot(q4, km_ref[...]) - bound).astype(BF16)
    acc = jnp.dot(pm, _with_ones(vm_ref[...]), preferred_element_type=F32)
    for j in range(seq // tk):
        p = jnp.exp2(_nt_dot(q4, k_ref[j * tk:(j + 1) * tk, :]) - bound).astype(BF16)
        acc = acc + jnp.dot(p, _with_ones(v_ref[j * tk:(j + 1) * tk, :]), preferred_element_type=F32)
    _diff_finish([acc[c * tq:(c + 1) * tq] for c in range(4)], sc_ref[0], gh_ref[...], o_ref)


def _diff_online_kernel(sc_ref, q_ref, k_ref, v_ref, km_ref, vm_ref, gh_ref, o_ref, m_ref, acc_ref,
                        *, tq, tk, seq):
    qms = _quarter_masked(q_ref[...])
    km = km_ref[...]
    vm = _with_ones(vm_ref[...])
    for c in range(4):
        s = _nt_dot(qms[c], km)
        m = jnp.max(s, axis=-1, keepdims=True)
        p = jnp.exp2(s - m)
        m_ref[c] = m
        acc_ref[c] = jnp.dot(p.astype(BF16), vm, preferred_element_type=F32)

    def body(j, carry):
        off = pl.multiple_of(j * tk, tk)
        kt = k_ref[pl.ds(off, tk), :]
        vt = _with_ones(v_ref[pl.ds(off, tk), :])
        for c in range(4):
            s = _nt_dot(qms[c], kt)
            m_old = m_ref[c]
            m_new = jnp.maximum(m_old, jnp.max(s, axis=-1, keepdims=True))
            p = jnp.exp2(s - m_new)
            acc_ref[c] = (jnp.exp2(m_old - m_new) * acc_ref[c]
                          + jnp.dot(p.astype(BF16), vt, preferred_element_type=F32))
            m_ref[c] = m_new
        return carry

    lax.fori_loop(0, seq // tk, body, 0)
    _diff_finish([acc_ref[c] for c in range(4)], sc_ref[0], gh_ref[...], o_ref)


def _diff(proj, proj_meta, scalars, gh, tq, tk, bounded):
    b, seq, _ = proj.shape
    n_pairs = DIFF_H // 2
    if bounded:
        kern = functools.partial(_diff_bounded_kernel, tq=tq, tk=tk, seq=seq)
        scratch = []
    else:
        kern = functools.partial(_diff_online_kernel, tq=tq, tk=tk, seq=seq)
        scratch = [pltpu.VMEM((4, tq, 1), F32), pltpu.VMEM((4, tq, 2 * LANES), F32)]
    return pl.pallas_call(
        kern,
        grid=(b, n_pairs, seq // tq),
        in_specs=[
            pl.BlockSpec(memory_space=pltpu.SMEM),
            pl.BlockSpec((None, tq, LANES), lambda bi, hp, i: (bi, i, G_QB + hp)),
            pl.BlockSpec((None, seq, LANES), lambda bi, hp, i: (bi, 0, G_KB + hp)),
            pl.BlockSpec((None, seq, LANES), lambda bi, hp, i: (bi, 0, G_VB + hp)),
            pl.BlockSpec((N_META, LANES), lambda bi, hp, i: (0, G_KB + hp)),
            pl.BlockSpec((N_META, LANES), lambda bi, hp, i: (0, G_VB + hp)),
            pl.BlockSpec((1, LANES), lambda bi, hp, i: (0, 0)),
        ],
        out_specs=pl.BlockSpec((None, tq, LANES), lambda bi, hp, i: (bi, i, hp)),
        out_shape=jax.ShapeDtypeStruct((b, seq, n_pairs * LANES), BF16),
        scratch_shapes=scratch,
        compiler_params=_cparams(("parallel", "parallel", "parallel")),
        name="diff_bounded" if bounded else "diff_online",
    )(scalars, proj, proj, proj, proj_meta, proj_meta, gh)


def _outproj_kernel(ma_ref, mb_ref, x_ref, wo_ref, g_ref, wrh_ref, wrl_ref, br_ref, tri_ref,
                    h1_ref, v_ref, idx_ref, gate_ref, rank_ref, cnt_ref, carry_ref, *, tm):
    half = wo_ref.shape[0] // 2

    @pl.when(pl.program_id(0) == 0)
    def _():
        carry_ref[...] = jnp.zeros_like(carry_ref)

    h1 = (x_ref[...]
          + jnp.dot(ma_ref[...], wo_ref[:half, :], preferred_element_type=F32)
          + jnp.dot(mb_ref[...], wo_ref[half:, :], preferred_element_type=F32))
    h1_ref[...] = h1
    v = h1 * lax.rsqrt(jnp.mean(h1 * h1, axis=-1, keepdims=True) + EPS) * g_ref[...]
    v_ref[...] = v
    vh = v.astype(BF16)
    vl = (v - vh.astype(F32)).astype(BF16)
    work = (_nt_dot(wrh_ref[...], vh) + _nt_dot(wrl_ref[...], vh) + _nt_dot(wrh_ref[...], vl)
            + br_ref[...])
    iota_e = lax.broadcasted_iota(jnp.int32, (N_EXPERTS, tm), 0)
    vals, idxs, sels = [], [], []
    for _ in range(TOP_K):
        mk = jnp.max(work, axis=0, keepdims=True)
        ik = jnp.min(jnp.where(work == mk, iota_e, N_EXPERTS), axis=0, keepdims=True)
        sel = iota_e == ik
        work = jnp.where(sel, -jnp.inf, work)
        vals.append(mk)
        idxs.append(ik)
        sels.append(sel)
    exps = [jnp.exp(vk - vals[0]) for vk in vals]
    denom = exps[0] + exps[1] + exps[2] + exps[3]
    gate_ref[...] = jnp.concatenate([e / denom for e in exps], axis=0)
    idx_ref[...] = jnp.concatenate(idxs, axis=0)
    cnt = jnp.zeros((N_EXPERTS, tm), F32)
    for sel in sels:
        cnt = cnt + jnp.where(sel, 1.0, 0.0)
    before = jnp.dot(cnt.astype(BF16), tri_ref[...], preferred_element_type=F32) + carry_ref[...]
    ranks = [jnp.sum(jnp.where(sel, before, 0.0), axis=0, keepdims=True) for sel in sels]
    rank_ref[...] = jnp.concatenate(ranks, axis=0).astype(jnp.int32)
    carry_ref[...] = carry_ref[...] + jnp.sum(cnt, axis=1, keepdims=True)
    cnt_ref[...] = jnp.broadcast_to(carry_ref[...], cnt_ref.shape)


def _outproj(mixed_a, mixed_b, x2, wo, g, wrh, wrl, br, tri, tm):
    t, d = x2.shape
    hw = mixed_a.shape[1]
    kern = functools.partial(_outproj_kernel, tm=tm)
    row = lambda i: (i, 0)
    fix = lambda i: (0, 0)
    col = lambda i: (0, i)
    return pl.pallas_call(
        kern,
        grid=(t // tm,),
        in_specs=[
            pl.BlockSpec((tm, hw), row), pl.BlockSpec((tm, hw), row), pl.BlockSpec((tm, d), row),
            pl.BlockSpec((d, d), fix), pl.BlockSpec((1, d), fix),
            pl.BlockSpec((N_EXPERTS, d), fix), pl.BlockSpec((N_EXPERTS, d), fix),
            pl.BlockSpec((N_EXPERTS, 1), fix), pl.BlockSpec((tm, tm), fix),
        ],
        out_specs=[
            pl.BlockSpec((tm, d), row), pl.BlockSpec((tm, d), row),
            pl.BlockSpec((TOP_K, tm), col), pl.BlockSpec((TOP_K, tm), col),
            pl.BlockSpec((TOP_K, tm), col), pl.BlockSpec((N_EXPERTS, LANES), fix),
        ],
        out_shape=[
            jax.ShapeDtypeStruct((t, d), F32), jax.ShapeDtypeStruct((t, d), F32),
            jax.ShapeDtypeStruct((TOP_K, t), jnp.int32), jax.ShapeDtypeStruct((TOP_K, t), F32),
            jax.ShapeDtypeStruct((TOP_K, t), jnp.int32),
            jax.ShapeDtypeStruct((N_EXPERTS, LANES), F32),
        ],
        scratch_shapes=[pltpu.VMEM((N_EXPERTS, 1), F32)],
        compiler_params=_cparams(("arbitrary",)), name="outproj",
    )(mixed_a, mixed_b, x2, wo, g, wrh, wrl, br, tri)


def _dispatch_kernel(fill_ref, dest_ref, v_ref, xs_hbm, zero_ref, sem, zsem, *, tmd, tme):
    n_tiles = xs_hbm.shape[0] // tme

    @pl.when(pl.program_id(0) == 0)
    def _():
        zero_ref[...] = jnp.zeros_like(zero_ref)

        def row_copy(r):
            return pltpu.make_async_copy(zero_ref.at[pl.ds(0, 1)], xs_hbm.at[pl.ds(r, 1)], zsem)

        def tile_copy(i):
            return pltpu.make_async_copy(zero_ref, xs_hbm.at[pl.ds(i * tme, tme)], zsem)

        def fill(start, wait):
            for e in range(N_EXPERTS):
                def one_row(r, carry):
                    row_copy(r).start() if start else row_copy(r).wait()
                    return carry
                lax.fori_loop(fill_ref[e], fill_ref[N_EXPERTS + e], one_row, 0)

            def one_tile(i, carry):
                tile_copy(i).start() if start else tile_copy(i).wait()
                return carry
            lax.fori_loop(fill_ref[2 * N_EXPERTS], n_tiles, one_tile, 0)

        fill(True, False)
        fill(False, True)

    def issue(t, carry):
        for k in range(TOP_K):
            pltpu.make_async_copy(v_ref.at[pl.ds(t, 1)],
                                  xs_hbm.at[pl.ds(dest_ref[TOP_K * t + k], 1)], sem).start(priority=k % 2)
        return carry

    lax.fori_loop(0, tmd, issue, 0, unroll=4)
    for k in range(TOP_K):
        pltpu.make_async_copy(v_ref, xs_hbm.at[pl.ds(0, tmd)], sem).wait()


def _dispatch(fill_meta, dest_flat, v, n_slots, tmd, tme):
    t, d = v.shape
    kern = functools.partial(_dispatch_kernel, tmd=tmd, tme=tme)
    grid_spec = pltpu.PrefetchScalarGridSpec(
        num_scalar_prefetch=1,
        grid=(t // tmd,),
        in_specs=[
            pl.BlockSpec((TOP_K * tmd,), lambda i, fm: (i,), memory_space=pltpu.SMEM),
            pl.BlockSpec((tmd, d), lambda i, fm: (i, 0)),
        ],
        out_specs=pl.BlockSpec(memory_space=pl.ANY),
        scratch_shapes=[pltpu.VMEM((tme, d), F32), pltpu.SemaphoreType.DMA, pltpu.SemaphoreType.DMA],
    )
    return pl.pallas_call(
        kern,
        grid_spec=grid_spec,
        out_shape=jax.ShapeDtypeStruct((n_slots, d), F32),
        compiler_params=_cparams(("arbitrary",)), name="dispatch",
    )(fill_meta, dest_flat, v)


def _experts_kernel(te_ref, ts_ref, tv_ref, xs_ref, w1_ref, b1_ref, w2_ref, b2_ref, ys_ref):
    del te_ref, ts_ref
    d_ff = w2_ref.shape[0]

    @pl.when(tv_ref[pl.program_id(0)] > 0)
    def _():
        a = jnp.dot(xs_ref[...], w1_ref[...], preferred_element_type=F32) + b1_ref[...]
        glu = jnp.minimum(a[:, :d_ff], SWIGLU_LIMIT)
        lin = jnp.clip(a[:, d_ff:], -SWIGLU_LIMIT, SWIGLU_LIMIT)
        act = glu * jax.nn.sigmoid(SWIGLU_ALPHA * glu) * (lin + 1.0)
        ys_ref[...] = jnp.dot(act, w2_ref[...], preferred_element_type=F32) + b2_ref[...]

    @pl.when(tv_ref[pl.program_id(0)] == 0)
    def _():
        ys_ref[...] = jnp.zeros_like(ys_ref)


def _experts(tile_expert, tile_src, tile_valid, xs, w1, b1, w2, b2, tm):
    n_slots, d = xs.shape
    two_ff = w1.shape[2]
    d_ff = two_ff // 2
    grid_spec = pltpu.PrefetchScalarGridSpec(
        num_scalar_prefetch=3,
        grid=(n_slots // tm,),
        in_specs=[
            pl.BlockSpec((tm, d), lambda i, te, ts, tv: (ts[i], 0)),
            pl.BlockSpec((None, d, two_ff), lambda i, te, ts, tv: (te[i], 0, 0)),
            pl.BlockSpec((None, 1, two_ff), lambda i, te, ts, tv: (te[i], 0, 0)),
            pl.BlockSpec((None, d_ff, d), lambda i, te, ts, tv: (te[i], 0, 0)),
            pl.BlockSpec((None, 1, d), lambda i, te, ts, tv: (te[i], 0, 0)),
        ],
        out_specs=pl.BlockSpec((tm, d), lambda i, te, ts, tv: (i, 0)),
    )
    return pl.pallas_call(
        _experts_kernel,
        grid_spec=grid_spec,
        out_shape=jax.ShapeDtypeStruct(xs.shape, F32),
        compiler_params=_cparams(("arbitrary",)), name="experts",
    )(tile_expert, tile_src, tile_valid, xs, w1, b1, w2, b2)


def _combine_kernel(dest_ref, dest_next_ref, gate_ref, h1_ref, ys_hbm, o_ref, buf_ref, sem, *, tmc):
    i = pl.program_id(0)
    n = pl.num_programs(0)

    def issue(d_ref, slot):
        def one(t, carry):
            for k in range(TOP_K):
                pltpu.make_async_copy(ys_hbm.at[pl.ds(d_ref[TOP_K * t + k], 1)],
                                      buf_ref.at[slot, k, pl.ds(t, 1)], sem.at[slot]).start(priority=k % 2)
            return carry
        lax.fori_loop(0, tmc, one, 0, unroll=4)

    @pl.when(i == 0)
    def _():
        issue(dest_ref, 0)

    slot = i % 2

    @pl.when(i + 1 < n)
    def _():
        issue(dest_next_ref, 1 - slot)

    pltpu.make_async_copy(buf_ref.at[slot], buf_ref.at[slot], sem.at[slot]).wait()
    acc = h1_ref[...]
    for k in range(TOP_K):
        acc = acc + gate_ref[:, k:k + 1] * buf_ref[slot, k]
    o_ref[...] = acc


def _combine(dest_flat, gate_tk, h1, ys, tmc):
    t, d = h1.shape
    n = t // tmc
    kern = functools.partial(_combine_kernel, tmc=tmc)
    return pl.pallas_call(
        kern,
        grid=(n,),
        in_specs=[
            pl.BlockSpec((TOP_K * tmc,), lambda i: (i,), memory_space=pltpu.SMEM),
            pl.BlockSpec((TOP_K * tmc,), lambda i: (jnp.minimum(i + 1, n - 1),), memory_space=pltpu.SMEM),
            pl.BlockSpec((tmc, TOP_K), lambda i: (i, 0)),
            pl.BlockSpec((tmc, d), lambda i: (i, 0)),
            pl.BlockSpec(memory_space=pl.ANY),
        ],
        out_specs=pl.BlockSpec((tmc, d), lambda i: (i, 0)),
        out_shape=jax.ShapeDtypeStruct((t, d), F32),
        scratch_shapes=[
            pltpu.VMEM((2, TOP_K, tmc, d), F32),
            pltpu.SemaphoreType.DMA((2,)),
        ],
        compiler_params=_cparams(("arbitrary",)), name="combine",
    )(dest_flat, dest_flat, gate_tk, h1, ys)


def _swap_halves(g, dh):
    return jnp.concatenate([g[dh // 2:], g[:dh // 2]])


def _rope_tables(pos, dh, gq, gk, q_scale):
    inv = ROPE_THETA ** (-jnp.arange(0, dh, 2, dtype=F32) / dh)
    ang = pos.astype(F32)[:, None] * inv[None, :]
    cos = jnp.concatenate([jnp.cos(ang), jnp.cos(ang)], axis=1)
    sin = jnp.concatenate([-jnp.sin(ang), jnp.sin(ang)], axis=1)
    rep = LANES // dh
    tile = lambda a: jnp.tile(a, (1, rep))
    gq, gk = gq.astype(F32), gk.astype(F32)
    return jnp.stack([
        tile(cos * gq[None]) * q_scale, tile(sin * _swap_halves(gq, dh)[None]) * q_scale,
        tile(cos * gk[None]), tile(sin * _swap_halves(gk, dh)[None]),
    ])


def _rope_matrix(dh):
    j = np.arange(LANES)
    swap = (j // dh) * dh + (j % dh + dh // 2) % dh
    m = np.zeros((2 * LANES, 2 * LANES), np.float32)
    m[swap, j] = 1.0
    m[LANES:, LANES:] = (j[:, None] // dh == j[None, :] // dh)
    return jnp.asarray(m, BF16)


def _pick_tile(n, pref):
    t = min(n, pref)
    while n % t:
        t //= 2
    return t


def kernel(x, meta_tokens, g_attn, w_in, g_q_swa, g_k_swa, g_q_diff, g_k_diff, sink_swa, lambda_q1, lambda_k1, lambda_q2, lambda_k2, g_diff_head, w_out, g_ffn, w_router, b_router, w_mlp1, b_mlp1, w_mlp2, b_mlp2):
    b, seq, d = x.shape
    assert g_attn.shape[0] == 1 and d % LANES == 0 and seq % 512 == 0
    t = b * seq
    lambda_init = 0.8 - 0.6 * math.exp(-0.3 * 0)

    wi = w_in[0]
    q_a, k_a, v_a, q_b, k_b, v_b = (wi[:, s:e] for s, e in
                                    ((0, 512), (512, 640), (640, 768), (768, 1280), (1280, 1792), (1792, 2304)))
    dup = lambda w: jnp.concatenate([w[:, :64], w[:, :64], w[:, 64:], w[:, 64:]], axis=1)
    w_wide = jnp.concatenate([q_a, dup(k_a), dup(v_a), q_b, k_b, v_b], axis=1).astype(BF16)
    mat_a, mat_b = _rope_matrix(SWA_DH), _rope_matrix(DIFF_DQ)

    def tables(pos):
        return jnp.concatenate([
            _rope_tables(pos, SWA_DH, g_q_swa[0], g_k_swa[0], SWA_DH ** -0.5 * LOG2E),
            _rope_tables(pos, DIFF_DQ, g_q_diff[0], g_k_diff[0], DIFF_DQ ** -0.5 * LOG2E)])

    tab_tok = tables(jnp.arange(N_META, N_META + seq))
    tab_meta = tables(jnp.arange(N_META))
    g_attn2 = g_attn[0].reshape(1, d).astype(F32)

    x2 = x.reshape(t, d)
    tm = _pick_tile(seq, 512)
    proj = _inproj(x2, g_attn2, w_wide, tab_tok, mat_a, mat_b, tm).reshape(b, seq, PROJ_W)
    proj_meta = _inproj(meta_tokens.astype(F32), g_attn2, w_wide, tab_meta, mat_a, mat_b, N_META)

    def score_bound(dh, gq, gk):
        return dh ** 0.5 * LOG2E * jnp.max(jnp.abs(gq.astype(F32))) * jnp.max(jnp.abs(gk.astype(F32)))

    bound_a = score_bound(SWA_DH, g_q_swa[0], g_k_swa[0])
    swa_scalars = jnp.concatenate([sink_swa[0].astype(F32) * LOG2E, bound_a[None]])
    mixed_a = lax.cond(
        bound_a <= MAX_SHIFT_BOUND,
        lambda: _swa(proj, proj_meta, swa_scalars, 256, 512, True),
        lambda: _swa(proj, proj_meta, swa_scalars, 256, 512, False))
    lam = (jnp.exp(jnp.sum(lambda_q1[0].astype(F32) * lambda_k1[0].astype(F32)))
           - jnp.exp(jnp.sum(lambda_q2[0].astype(F32) * lambda_k2[0].astype(F32))) + lambda_init)
    gh = (jnp.tile(g_diff_head[0].astype(F32), 2) * (1.0 - lambda_init)).reshape(1, LANES)
    bound_b = score_bound(DIFF_DQ, g_q_diff[0], g_k_diff[0])
    scalars = jnp.stack([lam, bound_b])
    mixed_b = lax.cond(
        bound_b <= MAX_SHIFT_BOUND,
        lambda: _diff(proj, proj_meta, scalars, gh, _pick_tile(seq, 512), _pick_tile(seq, 512), True),
        lambda: _diff(proj, proj_meta, scalars, gh, _pick_tile(seq, 512), _pick_tile(seq, 512), False))

    wr = w_router[0].T.astype(F32)
    wrh = wr.astype(BF16)
    wrl = (wr - wrh.astype(F32)).astype(BF16)
    tri = jnp.asarray(np.triu(np.ones((tm, tm), np.float32), 1), BF16)
    h1, v, idx_t, gate_t, rank_t, cnt = _outproj(
        mixed_a.reshape(t, -1), mixed_b.reshape(t, -1), x2, w_out[0].astype(BF16),
        g_ffn[0].reshape(1, d).astype(F32), wrh, wrl, b_router[0].reshape(N_EXPERTS, 1).astype(F32),
        tri, tm)

    tme = 512
    counts = cnt[:, 0].astype(jnp.int32)
    padded = (counts + tme - 1) // tme * tme
    pad_end = jnp.cumsum(padded)
    pad_start = pad_end - padded
    start_of = jnp.sum(jnp.where(idx_t[None] == jnp.arange(N_EXPERTS)[:, None, None],
                                 pad_start[:, None, None], 0), axis=0)
    dest_flat = (start_of + rank_t).T.reshape(-1)
    gate_tk = gate_t.T
    n_tiles = -(-(TOP_K * t) // tme) + N_EXPERTS
    n_slots = n_tiles * tme
    tiles = jnp.arange(n_tiles, dtype=jnp.int32)
    n_valid = pad_end[-1] // tme
    tile_src = jnp.minimum(tiles, n_valid - 1)
    tile_expert = jnp.minimum(jnp.sum(pad_end[None, :] <= (tile_src * tme)[:, None], axis=1),
                              N_EXPERTS - 1).astype(jnp.int32)
    tile_valid = (tiles < n_valid).astype(jnp.int32)

    fill_meta = jnp.concatenate([pad_start + counts, pad_end, n_valid[None]]).astype(jnp.int32)
    xs = _dispatch(fill_meta, dest_flat, v, n_slots, _pick_tile(t, 512), tme)
    ys = _experts(tile_expert, tile_src, tile_valid, xs,
                  w_mlp1[0], b_mlp1[0].reshape(N_EXPERTS, 1, -1).astype(F32),
                  w_mlp2[0], b_mlp2[0].reshape(N_EXPERTS, 1, -1).astype(F32), tme)
    out = _combine(dest_flat, gate_tk, h1, ys, _pick_tile(t, 256))
    return out.reshape(b, seq, d)
```

```python
import functools
import math

import numpy as np
import jax
import jax.numpy as jnp
from jax import lax
from jax.experimental import pallas as pl
from jax.experimental.pallas import tpu as pltpu

F32 = jnp.float32
BF16 = jnp.bfloat16

N_META = 16
WINDOW = 128
ROPE_THETA = 10000.0
EPS = 1e-6
NEG_INF = -1e30
SWA_HQ, SWA_HKV, SWA_DH = 8, 2, 64
DIFF_H, DIFF_DQ, DIFF_DV = 8, 32, 64
N_EXPERTS = 32
TOP_K = 4
SWIGLU_LIMIT = 7.0
SWIGLU_ALPHA = 1.702
LOG2E = math.log2(math.e)
LANES = 128
VMEM_LIMIT = 56 * 1024 * 1024
MAX_SHIFT_BOUND = 60.0

G_QA, G_KA, G_VA, G_QB, G_KB, G_VB = 0, 4, 6, 8, 12, 16
N_GROUPS = 20
PROJ_W = N_GROUPS * LANES


def _nt_dot(a, b):
    return lax.dot_general(a, b, (((1,), (1,)), ((), ())), preferred_element_type=F32)


def _cparams(sem):
    return pltpu.CompilerParams(dimension_semantics=sem, vmem_limit_bytes=VMEM_LIMIT)


def _inproj_kernel(x_ref, g_ref, w_ref, tab_ref, ma_ref, mb_ref, o_ref):
    x = x_ref[...]
    ms = jnp.mean(x * x, axis=-1, keepdims=True)
    u = (x * lax.rsqrt(ms + EPS) * g_ref[...]).astype(BF16)

    def rope_group(y, mat, inv_dh, c, s):
        z = jnp.concatenate([y.astype(BF16), (y * y).astype(BF16)], axis=1)
        r = jnp.dot(z, mat, preferred_element_type=F32)
        sw, ss = r[:, :LANES], r[:, LANES:]
        return lax.rsqrt(ss * inv_dh + EPS) * (y * c + sw * s)

    chunk = 4 * LANES
    for c0 in range(0, N_GROUPS, 4):
        y4 = jnp.dot(u, w_ref[:, c0 * LANES:c0 * LANES + chunk], preferred_element_type=F32)
        for j in range(4):
            grp = c0 + j
            y = y4[:, j * LANES:(j + 1) * LANES]
            if G_QA <= grp < G_KA:
                y = rope_group(y, ma_ref[...], 1.0 / SWA_DH, tab_ref[0], tab_ref[1])
            elif G_KA <= grp < G_VA:
                y = rope_group(y, ma_ref[...], 1.0 / SWA_DH, tab_ref[2], tab_ref[3])
            elif G_QB <= grp < G_KB:
                y = rope_group(y, mb_ref[...], 1.0 / DIFF_DQ, tab_ref[4], tab_ref[5])
            elif G_KB <= grp < G_VB:
                y = rope_group(y, mb_ref[...], 1.0 / DIFF_DQ, tab_ref[6], tab_ref[7])
            o_ref[:, grp * LANES:(grp + 1) * LANES] = y.astype(BF16)


def _inproj(x2, g, w, tab, mat_a, mat_b, tm):
    rows, d = x2.shape
    n_tab = tab.shape[1] // tm
    return pl.pallas_call(
        _inproj_kernel,
        grid=(rows // tm,),
        in_specs=[
            pl.BlockSpec((tm, d), lambda i: (i, 0)),
            pl.BlockSpec((1, d), lambda i: (0, 0)),
            pl.BlockSpec((d, PROJ_W), lambda i: (0, 0)),
            pl.BlockSpec((8, tm, LANES), lambda i: (0, i % n_tab, 0)),
            pl.BlockSpec((2 * LANES, 2 * LANES), lambda i: (0, 0)),
            pl.BlockSpec((2 * LANES, 2 * LANES), lambda i: (0, 0)),
        ],
        out_specs=pl.BlockSpec((tm, PROJ_W), lambda i: (i, 0)),
        out_shape=jax.ShapeDtypeStruct((rows, PROJ_W), BF16),
        compiler_params=_cparams(("parallel",)), name="inproj",
    )(x2, g, w, tab, mat_a, mat_b)


def _swa_window(tq, win, seq):
    t0 = pl.program_id(1) * tq
    start = pl.multiple_of(jnp.clip(t0 - WINDOW, 0, seq - win), LANES)
    qpos = t0 + lax.broadcasted_iota(jnp.int32, (tq, win), 0)
    kpos = start + lax.broadcasted_iota(jnp.int32, (tq, win), 1)
    return start, jnp.abs(qpos - kpos) <= WINDOW


def _swa_bounded_kernel(sc_ref, q_ref, k_ref, v_ref, km_ref, vm_ref, o_ref, *, tq, win, seq):
    start, ok = _swa_window(tq, win, seq)
    bound = sc_ref[SWA_HQ]
    low_half = lax.broadcasted_iota(jnp.int32, (tq, LANES), 1) < SWA_DH
    n_rep = SWA_HQ // SWA_HKV
    for hk in range(SWA_HKV):
        cols = slice(hk * LANES, (hk + 1) * LANES)
        qs = []
        for pair in range(2):
            qp = q_ref[:, (2 * hk + pair) * LANES:(2 * hk + pair + 1) * LANES]
            qs += [jnp.where(low_half, qp, jnp.zeros_like(qp)), jnp.where(low_half, jnp.zeros_like(qp), qp)]
        q4 = jnp.concatenate(qs, axis=0)
        s = _nt_dot(q4, k_ref[pl.ds(start, win), cols])
        sm = _nt_dot(q4, km_ref[:, cols])
        ps, pms, sink_p = [], [], []
        for r in range(n_rep):
            sink = sc_ref[n_rep * hk + r]
            shift = jnp.maximum(bound, sink)
            rows = slice(r * tq, (r + 1) * tq)
            ps.append(jnp.exp2(jnp.where(ok, s[rows] - shift, NEG_INF)).astype(BF16))
            pms.append(jnp.exp2(sm[rows] - shift).astype(BF16))
            sink_p.append(jnp.exp2(jnp.full((tq, 1), sink - shift, F32)))
        acc = (jnp.dot(jnp.concatenate(ps, axis=0), _with_ones(v_ref[pl.ds(start, win), cols]),
                       preferred_element_type=F32)
               + jnp.dot(jnp.concatenate(pms, axis=0), _with_ones(vm_ref[:, cols]),
                         preferred_element_type=F32))
        outs = [acc[r * tq:(r + 1) * tq, :LANES] / (acc[r * tq:(r + 1) * tq, LANES:LANES + 1] + sink_p[r])
                for r in range(n_rep)]
        for pair in range(2):
            gq = 2 * hk + pair
            o_ref[:, gq * LANES:(gq + 1) * LANES] = jnp.where(
                low_half, outs[2 * pair], outs[2 * pair + 1]).astype(BF16)


def _swa_online_kernel(sink_ref, q_ref, k_ref, v_ref, km_ref, vm_ref, o_ref, *, tq, win, seq):
    start, ok = _swa_window(tq, win, seq)
    low_half = lax.broadcasted_iota(jnp.int32, (tq, LANES), 1) < SWA_DH
    for hk in range(SWA_HKV):
        cols = slice(hk * LANES, (hk + 1) * LANES)
        kw = k_ref[pl.ds(start, win), cols]
        vw = v_ref[pl.ds(start, win), cols]
        km = km_ref[:, cols]
        vm = vm_ref[:, cols]
        for pair in range(2):
            gq = 2 * hk + pair
            qp = q_ref[:, gq * LANES:(gq + 1) * LANES]
            outs = []
            for half in range(2):
                sink = sink_ref[2 * gq + half]
                qm = jnp.where(low_half == (half == 0), qp, jnp.zeros_like(qp))
                s = jnp.where(ok, _nt_dot(qm, kw), NEG_INF)
                sm = _nt_dot(qm, km)
                m = jnp.maximum(jnp.maximum(jnp.max(s, axis=-1, keepdims=True),
                                            jnp.max(sm, axis=-1, keepdims=True)), sink)
                p = jnp.exp2(s - m)
                pm = jnp.exp2(sm - m)
                l = (jnp.sum(p, axis=-1, keepdims=True) + jnp.sum(pm, axis=-1, keepdims=True)
                     + jnp.exp2(sink - m))
                o = (jnp.dot(p.astype(BF16), vw, preferred_element_type=F32)
                     + jnp.dot(pm.astype(BF16), vm, preferred_element_type=F32))
                outs.append(o / l)
            o_ref[:, gq * LANES:(gq + 1) * LANES] = jnp.where(low_half, outs[0], outs[1]).astype(BF16)


def _swa(proj, proj_meta, sink2, tq, win, bounded):
    b, seq, _ = proj.shape
    kern = functools.partial(_swa_bounded_kernel if bounded else _swa_online_kernel, tq=tq, win=win, seq=seq)
    return pl.pallas_call(
        kern,
        grid=(b, seq // tq),
        in_specs=[
            pl.BlockSpec(memory_space=pltpu.SMEM),
            pl.BlockSpec((None, tq, 4 * LANES), lambda bi, i: (bi, i, G_QA // 4)),
            pl.BlockSpec((None, seq, 2 * LANES), lambda bi, i: (bi, 0, G_KA // 2)),
            pl.BlockSpec((None, seq, 2 * LANES), lambda bi, i: (bi, 0, G_VA // 2)),
            pl.BlockSpec((N_META, 2 * LANES), lambda bi, i: (0, G_KA // 2)),
            pl.BlockSpec((N_META, 2 * LANES), lambda bi, i: (0, G_VA // 2)),
        ],
        out_specs=pl.BlockSpec((None, tq, 4 * LANES), lambda bi, i: (bi, i, 0)),
        out_shape=jax.ShapeDtypeStruct((b, seq, 4 * LANES), BF16),
        compiler_params=_cparams(("parallel", "parallel")),
        name="swa_bounded" if bounded else "swa_online",
    )(sink2, proj, proj, proj, proj_meta, proj_meta)


def _quarter_masked(q):
    lane = lax.broadcasted_iota(jnp.int32, q.shape, 1)
    return [jnp.where((lane >= DIFF_DQ * c) & (lane < DIFF_DQ * (c + 1)), q, jnp.zeros_like(q))
            for c in range(4)]


def _with_ones(v):
    return jnp.concatenate([v, jnp.ones_like(v)], axis=1)


def _diff_finish(accs, lam, gh, o_ref):
    def normalized(a):
        return a[:, :LANES] / a[:, LANES:LANES + 1]

    low = lax.broadcasted_iota(jnp.int32, (accs[0].shape[0], LANES), 1) < DIFF_DV
    o = jnp.where(low, normalized(accs[0]) - lam * normalized(accs[1]),
                  normalized(accs[2]) - lam * normalized(accs[3]))
    o2 = o * o
    ss_lo = jnp.sum(jnp.where(low, o2, 0.0), axis=-1, keepdims=True)
    ss_hi = jnp.sum(jnp.where(low, 0.0, o2), axis=-1, keepdims=True)
    rs = lax.rsqrt(jnp.where(low, ss_lo, ss_hi) * (1.0 / DIFF_DV) + 1e-5)
    o_ref[...] = (o * rs * gh).astype(BF16)


def _diff_bounded_kernel(sc_ref, q_ref, k_ref, v_ref, km_ref, vm_ref, gh_ref, o_ref, *, tq, tk, seq):
    bound = sc_ref[1]
    q4 = jnp.concatenate(_quarter_masked(q_ref[...]), axis=0)
    pm = jnp.exp2(_nt_dot(q4, km_ref[...]) - bound).astype(BF16)
    acc = jnp.dot(pm, _with_ones(vm_ref[...]), preferred_element_type=F32)
    for j in range(seq // tk):
        p = jnp.exp2(_nt_dot(q4, k_ref[j * tk:(j + 1) * tk, :]) - bound).astype(BF16)
        acc = acc + jnp.dot(p, _with_ones(v_ref[j * tk:(j + 1) * tk, :]), preferred_element_type=F32)
    _diff_finish([acc[c * tq:(c + 1) * tq] for c in range(4)], sc_ref[0], gh_ref[...], o_ref)


def _diff_online_kernel(sc_ref, q_ref, k_ref, v_ref, km_ref, vm_ref, gh_ref, o_ref, m_ref, acc_ref,
                        *, tq, tk, seq):
    qms = _quarter_masked(q_ref[...])
    km = km_ref[...]
    vm = _with_ones(vm_ref[...])
    for c in range(4):
        s = _nt_dot(qms[c], km)
        m = jnp.max(s, axis=-1, keepdims=True)
        p = jnp.exp2(s - m)
        m_ref[c] = m
        acc_ref[c] = jnp.dot(p.astype(BF16), vm, preferred_element_type=F32)

    def body(j, carry):
        off = pl.multiple_of(j * tk, tk)
        kt = k_ref[pl.ds(off, tk), :]
        vt = _with_ones(v_ref[pl.ds(off, tk), :])
        for c in range(4):
            s = _nt_dot(qms[c], kt)
            m_old = m_ref[c]
            m_new = jnp.maximum(m_old, jnp.max(s, axis=-1, keepdims=True))
            p = jnp.exp2(s - m_new)
            acc_ref[c] = (jnp.exp2(m_old - m_new) * acc_ref[c]
                          + jnp.dot(p.astype(BF16), vt, preferred_element_type=F32))
            m_ref[c] = m_new
        return carry

    lax.fori_loop(0, seq // tk, body, 0)
    _diff_finish([acc_ref[c] for c in range(4)], sc_ref[0], gh_ref[...], o_ref)


def _diff(proj, proj_meta, scalars, gh, tq, tk, bounded):
    b, seq, _ = proj.shape
    n_pairs = DIFF_H // 2
    if bounded:
        kern = functools.partial(_diff_bounded_kernel, tq=tq, tk=tk, seq=seq)
        scratch = []
    else:
        kern = functools.partial(_diff_online_kernel, tq=tq, tk=tk, seq=seq)
        scratch = [pltpu.VMEM((4, tq, 1), F32), pltpu.VMEM((4, tq, 2 * LANES), F32)]
    return pl.pallas_call(
        kern,
        grid=(b, n_pairs, seq // tq),
        in_specs=[
            pl.BlockSpec(memory_space=pltpu.SMEM),
            pl.BlockSpec((None, tq, LANES), lambda bi, hp, i: (bi, i, G_QB + hp)),
            pl.BlockSpec((None, seq, LANES), lambda bi, hp, i: (bi, 0, G_KB + hp)),
            pl.BlockSpec((None, seq, LANES), lambda bi, hp, i: (bi, 0, G_VB + hp)),
            pl.BlockSpec((N_META, LANES), lambda bi, hp, i: (0, G_KB + hp)),
            pl.BlockSpec((N_META, LANES), lambda bi, hp, i: (0, G_VB + hp)),
            pl.BlockSpec((1, LANES), lambda bi, hp, i: (0, 0)),
        ],
        out_specs=pl.BlockSpec((None, tq, LANES), lambda bi, hp, i: (bi, i, hp)),
        out_shape=jax.ShapeDtypeStruct((b, seq, n_pairs * LANES), BF16),
        scratch_shapes=scratch,
        compiler_params=_cparams(("parallel", "parallel", "parallel")),
        name="diff_bounded" if bounded else "diff_online",
    )(scalars, proj, proj, proj, proj_meta, proj_meta, gh)


def _outproj_kernel(ma_ref, mb_ref, x_ref, wo_ref, g_ref, wrh_ref, wrl_ref, br_ref, tri_ref,
                    h1_ref, v_ref, idx_ref, gate_ref, rank_ref, cnt_ref, run_ref, carry_ref, *, tm):
    half = wo_ref.shape[0] // 2

    @pl.when(pl.program_id(0) == 0)
    def _():
        carry_ref[...] = jnp.zeros_like(carry_ref)

    h1 = (x_ref[...]
          + jnp.dot(ma_ref[...], wo_ref[:half, :], preferred_element_type=F32)
          + jnp.dot(mb_ref[...], wo_ref[half:, :], preferred_element_type=F32))
    h1_ref[...] = h1
    v = h1 * lax.rsqrt(jnp.mean(h1 * h1, axis=-1, keepdims=True) + EPS) * g_ref[...]
    v_ref[...] = v
    vh = v.astype(BF16)
    vl = (v - vh.astype(F32)).astype(BF16)
    work = (_nt_dot(wrh_ref[...], vh) + _nt_dot(wrl_ref[...], vh) + _nt_dot(wrh_ref[...], vl)
            + br_ref[...])
    iota_e = lax.broadcasted_iota(jnp.int32, (N_EXPERTS, tm), 0)
    vals, idxs, sels = [], [], []
    for _ in range(TOP_K):
        mk = jnp.max(work, axis=0, keepdims=True)
        ik = jnp.min(jnp.where(work == mk, iota_e, N_EXPERTS), axis=0, keepdims=True)
        sel = iota_e == ik
        work = jnp.where(sel, -jnp.inf, work)
        vals.append(mk)
        idxs.append(ik)
        sels.append(sel)
    exps = [jnp.exp(vk - vals[0]) for vk in vals]
    denom = exps[0] + exps[1] + exps[2] + exps[3]
    gate_ref[...] = jnp.concatenate([e / denom for e in exps], axis=0)
    idx_ref[...] = jnp.concatenate(idxs, axis=0)
    cnt = jnp.zeros((N_EXPERTS, tm), F32)
    for sel in sels:
        cnt = cnt + jnp.where(sel, 1.0, 0.0)
    before = jnp.dot(cnt.astype(BF16), tri_ref[...], preferred_element_type=F32) + carry_ref[...]
    ranks = [jnp.sum(jnp.where(sel, before, 0.0), axis=0, keepdims=True) for sel in sels]
    rank_ref[...] = jnp.concatenate(ranks, axis=0).astype(jnp.int32)
    run = carry_ref[...]
    for j in range(run_ref.shape[0]):
        run_ref[j] = jnp.broadcast_to(run, run_ref.shape[1:])
        run = run + jnp.sum(cnt[:, j * COMBINE_TOKENS:(j + 1) * COMBINE_TOKENS], axis=1, keepdims=True)
    carry_ref[...] = run
    cnt_ref[...] = jnp.broadcast_to(run, cnt_ref.shape)


def _outproj(mixed_a, mixed_b, x2, wo, g, wrh, wrl, br, tri, tm, row0, t):
    d = x2.shape[1]
    hw = mixed_a.shape[1]
    kern = functools.partial(_outproj_kernel, tm=tm)
    blk0 = row0 // tm
    src = lambda i: (i + blk0, 0)
    row = lambda i: (i, 0)
    fix = lambda i: (0, 0)
    col = lambda i: (0, i)
    return pl.pallas_call(
        kern,
        grid=(t // tm,),
        in_specs=[
            pl.BlockSpec((tm, hw), src), pl.BlockSpec((tm, hw), src), pl.BlockSpec((tm, d), src),
            pl.BlockSpec((d, d), fix), pl.BlockSpec((1, d), fix),
            pl.BlockSpec((N_EXPERTS, d), fix), pl.BlockSpec((N_EXPERTS, d), fix),
            pl.BlockSpec((N_EXPERTS, 1), fix), pl.BlockSpec((tm, tm), fix),
        ],
        out_specs=[
            pl.BlockSpec((tm, d), row), pl.BlockSpec((tm, d), row),
            pl.BlockSpec((TOP_K, tm), col), pl.BlockSpec((TOP_K, tm), col),
            pl.BlockSpec((TOP_K, tm), col), pl.BlockSpec((N_EXPERTS, LANES), fix),
            pl.BlockSpec((tm // COMBINE_TOKENS, N_EXPERTS, LANES), lambda i: (i, 0, 0)),
        ],
        out_shape=[
            jax.ShapeDtypeStruct((t, d), F32), jax.ShapeDtypeStruct((t, d), F32),
            jax.ShapeDtypeStruct((TOP_K, t), jnp.int32), jax.ShapeDtypeStruct((TOP_K, t), F32),
            jax.ShapeDtypeStruct((TOP_K, t), jnp.int32),
            jax.ShapeDtypeStruct((N_EXPERTS, LANES), F32),
            jax.ShapeDtypeStruct((t // COMBINE_TOKENS, N_EXPERTS, LANES), F32),
        ],
        scratch_shapes=[pltpu.VMEM((N_EXPERTS, 1), F32)],
        compiler_params=_cparams(("arbitrary",)), name="outproj",
    )(mixed_a, mixed_b, x2, wo, g, wrh, wrl, br, tri)


ZERO_ROWS = 64


def _zero_fill(fill_ref, xs_hbm, zero_ref, zsem, tme):
    n_tiles = xs_hbm.shape[0] // tme
    zero_ref[...] = jnp.zeros_like(zero_ref)

    def row_copy(r):
        return pltpu.make_async_copy(zero_ref.at[pl.ds(0, 1)], xs_hbm.at[pl.ds(r, 1)], zsem)

    def chunk_copy(c):
        return pltpu.make_async_copy(zero_ref, xs_hbm.at[pl.ds(c * ZERO_ROWS, ZERO_ROWS)], zsem)

    def fill(start):
        for e in range(N_EXPERTS):
            def one_row(r, carry):
                row_copy(r).start() if start else row_copy(r).wait()
                return carry
            lax.fori_loop(fill_ref[e], fill_ref[N_EXPERTS + e], one_row, 0)

        def one_chunk(c, carry):
            chunk_copy(c).start() if start else chunk_copy(c).wait()
            return carry
        per_tile = tme // ZERO_ROWS
        lax.fori_loop(fill_ref[2 * N_EXPERTS] * per_tile, n_tiles * per_tile, one_chunk, 0)

    fill(True)
    fill(False)


def _issue_rows(dest_ref, v_ref, xs_hbm, sem, t):
    for k in range(TOP_K):
        pltpu.make_async_copy(v_ref.at[pl.ds(t, 1)],
                              xs_hbm.at[pl.ds(dest_ref[TOP_K * t + k], 1)], sem).start(priority=k % 2)


def _wait_rows(v_ref, xs_hbm, sem):
    for _ in range(TOP_K):
        pltpu.make_async_copy(v_ref, xs_hbm.at[pl.ds(0, v_ref.shape[0])], sem).wait()


def _dispatch_kernel(fill_ref, dest_ref, v_ref, xs_hbm, zero_ref, sem, zsem, *, tmd, tme):
    @pl.when(pl.program_id(0) == 0)
    def _():
        _zero_fill(fill_ref, xs_hbm, zero_ref, zsem, tme)

    def issue(t, carry):
        _issue_rows(dest_ref, v_ref, xs_hbm, sem, t)
        return carry

    lax.fori_loop(0, tmd, issue, 0, unroll=4)
    _wait_rows(v_ref, xs_hbm, sem)


def _dispatch(fill_meta, dest_flat, v, n_slots, tmd, tme):
    t, d = v.shape
    kern = functools.partial(_dispatch_kernel, tmd=tmd, tme=tme)
    grid_spec = pltpu.PrefetchScalarGridSpec(
        num_scalar_prefetch=1,
        grid=(t // tmd,),
        in_specs=[
            pl.BlockSpec((TOP_K * tmd,), lambda i, fm: (i,), memory_space=pltpu.SMEM),
            pl.BlockSpec((tmd, d), lambda i, fm: (i, 0)),
        ],
        out_specs=pl.BlockSpec(memory_space=pl.ANY),
        scratch_shapes=[pltpu.VMEM((ZERO_ROWS, d), F32), pltpu.SemaphoreType.DMA, pltpu.SemaphoreType.DMA],
    )
    return pl.pallas_call(
        kern,
        grid_spec=grid_spec,
        out_shape=jax.ShapeDtypeStruct((n_slots, d), F32),
        compiler_params=_cparams(("arbitrary",)), name="dispatch",
    )(fill_meta, dest_flat, v)


def _expert_mlp(xs_ref, w1_ref, b1_ref, w2_ref, b2_ref, ys_ref):
    d_ff = w2_ref.shape[0]
    a = jnp.dot(xs_ref[...], w1_ref[...], preferred_element_type=F32) + b1_ref[...]
    glu = jnp.minimum(a[:, :d_ff], SWIGLU_LIMIT)
    lin = jnp.clip(a[:, d_ff:], -SWIGLU_LIMIT, SWIGLU_LIMIT)
    act = glu * jax.nn.sigmoid(SWIGLU_ALPHA * glu) * (lin + 1.0)
    ys_ref[...] = jnp.dot(act, w2_ref[...], preferred_element_type=F32) + b2_ref[...]


def _experts_kernel(te_ref, ts_ref, tv_ref, xs_ref, w1_ref, b1_ref, w2_ref, b2_ref, ys_ref):
    del te_ref, ts_ref

    @pl.when(tv_ref[pl.program_id(0)] > 0)
    def _():
        _expert_mlp(xs_ref, w1_ref, b1_ref, w2_ref, b2_ref, ys_ref)

    @pl.when(tv_ref[pl.program_id(0)] == 0)
    def _():
        ys_ref[...] = jnp.zeros_like(ys_ref)


def _experts(tile_expert, tile_src, tile_valid, xs, w1, b1, w2, b2, tm):
    n_slots, d = xs.shape
    two_ff = w1.shape[2]
    d_ff = two_ff // 2
    grid_spec = pltpu.PrefetchScalarGridSpec(
        num_scalar_prefetch=3,
        grid=(n_slots // tm,),
        in_specs=[
            pl.BlockSpec((tm, d), lambda i, te, ts, tv: (ts[i], 0)),
            pl.BlockSpec((None, d, two_ff), lambda i, te, ts, tv: (te[i], 0, 0)),
            pl.BlockSpec((None, 1, two_ff), lambda i, te, ts, tv: (te[i], 0, 0)),
            pl.BlockSpec((None, d_ff, d), lambda i, te, ts, tv: (te[i], 0, 0)),
            pl.BlockSpec((None, 1, d), lambda i, te, ts, tv: (te[i], 0, 0)),
        ],
        out_specs=pl.BlockSpec((tm, d), lambda i, te, ts, tv: (i, 0)),
    )
    return pl.pallas_call(
        _experts_kernel, grid_spec=grid_spec, out_shape=jax.ShapeDtypeStruct(xs.shape, F32),
        compiler_params=_cparams(("arbitrary",)), name="experts",
    )(tile_expert, tile_src, tile_valid, xs, w1, b1, w2, b2)


COMBINE_TOKENS = 256
RUN_CHUNK = 8
COMBINE_ROWS = COMBINE_TOKENS * TOP_K + N_EXPERTS * 2 * RUN_CHUNK


def _combine_kernel(meta_ref, meta_next_ref, pos_ref, gate_ref, h1_ref, ys_hbm, o_ref, buf_ref, sem):
    i = pl.program_id(0)
    n = pl.num_programs(0)

    def for_chunks(m_ref, slot, fn):
        for e in range(N_EXPERTS):
            def one(j, carry):
                src = pl.multiple_of(m_ref[e] + RUN_CHUNK * j, RUN_CHUNK)
                dst = pl.multiple_of(m_ref[2 * N_EXPERTS + e] + RUN_CHUNK * j, RUN_CHUNK)
                fn(pltpu.make_async_copy(ys_hbm.at[pl.ds(src, RUN_CHUNK)],
                                         buf_ref.at[slot, pl.ds(dst, RUN_CHUNK)], sem.at[slot]))
                return carry
            lax.fori_loop(0, m_ref[N_EXPERTS + e], one, 0)

    start = lambda c: c.start()

    @pl.when(i == 0)
    def _():
        buf_ref[...] = jnp.zeros_like(buf_ref)
        for_chunks(meta_ref, 0, start)

    slot = i % 2

    @pl.when(i + 1 < n)
    def _():
        for_chunks(meta_next_ref, 1 - slot, start)

    for_chunks(meta_ref, slot, lambda c: c.wait())
    rows = buf_ref[slot].astype(BF16)
    col = lax.broadcasted_iota(jnp.int32, (pos_ref.shape[0], COMBINE_ROWS), 1)
    g = jnp.zeros(col.shape, F32)
    for k in range(TOP_K):
        g = g + jnp.where(col == pos_ref[:, k:k + 1], gate_ref[:, k:k + 1], 0.0)
    g_hi = g.astype(BF16)
    g_lo = (g - g_hi.astype(F32)).astype(BF16)
    o_ref[...] = (h1_ref[...] + jnp.dot(g_hi, rows, preferred_element_type=F32)
                  + jnp.dot(g_lo, rows, preferred_element_type=F32))


def _combine(meta_flat, pos_tk, gate_tk, h1, ys):
    t, d = h1.shape
    tmc = COMBINE_TOKENS
    n = t // tmc
    return pl.pallas_call(
        _combine_kernel,
        grid=(n,),
        in_specs=[
            pl.BlockSpec((LANES,), lambda i: (i,), memory_space=pltpu.SMEM),
            pl.BlockSpec((LANES,), lambda i: (jnp.minimum(i + 1, n - 1),), memory_space=pltpu.SMEM),
            pl.BlockSpec((tmc, TOP_K), lambda i: (i, 0)),
            pl.BlockSpec((tmc, TOP_K), lambda i: (i, 0)),
            pl.BlockSpec((tmc, d), lambda i: (i, 0)),
            pl.BlockSpec(memory_space=pl.ANY),
        ],
        out_specs=pl.BlockSpec((tmc, d), lambda i: (i, 0)),
        out_shape=jax.ShapeDtypeStruct((t, d), F32),
        scratch_shapes=[
            pltpu.VMEM((2, COMBINE_ROWS, d), F32),
            pltpu.SemaphoreType.DMA((2,)),
        ],
        compiler_params=_cparams(("arbitrary",)), name="combine",
    )(meta_flat, meta_flat, pos_tk, gate_tk, h1, ys)


def _swap_halves(g, dh):
    return jnp.concatenate([g[dh // 2:], g[:dh // 2]])


def _rope_tables(pos, dh, gq, gk, q_scale):
    inv = ROPE_THETA ** (-jnp.arange(0, dh, 2, dtype=F32) / dh)
    ang = pos.astype(F32)[:, None] * inv[None, :]
    cos = jnp.concatenate([jnp.cos(ang), jnp.cos(ang)], axis=1)
    sin = jnp.concatenate([-jnp.sin(ang), jnp.sin(ang)], axis=1)
    rep = LANES // dh
    tile = lambda a: jnp.tile(a, (1, rep))
    gq, gk = gq.astype(F32), gk.astype(F32)
    return jnp.stack([
        tile(cos * gq[None]) * q_scale, tile(sin * _swap_halves(gq, dh)[None]) * q_scale,
        tile(cos * gk[None]), tile(sin * _swap_halves(gk, dh)[None]),
    ])


def _rope_matrix(dh):
    j = np.arange(LANES)
    swap = (j // dh) * dh + (j % dh + dh // 2) % dh
    m = np.zeros((2 * LANES, 2 * LANES), np.float32)
    m[swap, j] = 1.0
    m[LANES:, LANES:] = (j[:, None] // dh == j[None, :] // dh)
    return jnp.asarray(m, BF16)


def _pick_tile(n, pref):
    t = min(n, pref)
    while n % t:
        t //= 2
    return t


def kernel(x, meta_tokens, g_attn, w_in, g_q_swa, g_k_swa, g_q_diff, g_k_diff, sink_swa, lambda_q1, lambda_k1, lambda_q2, lambda_k2, g_diff_head, w_out, g_ffn, w_router, b_router, w_mlp1, b_mlp1, w_mlp2, b_mlp2):
    b, seq, d = x.shape
    assert g_attn.shape[0] == 1 and d % LANES == 0 and seq % 512 == 0
    t = b * seq
    lambda_init = 0.8 - 0.6 * math.exp(-0.3 * 0)

    wi = w_in[0]
    q_a, k_a, v_a, q_b, k_b, v_b = (wi[:, s:e] for s, e in
                                    ((0, 512), (512, 640), (640, 768), (768, 1280), (1280, 1792), (1792, 2304)))
    dup = lambda w: jnp.concatenate([w[:, :64], w[:, :64], w[:, 64:], w[:, 64:]], axis=1)
    w_wide = jnp.concatenate([q_a, dup(k_a), dup(v_a), q_b, k_b, v_b], axis=1).astype(BF16)
    mat_a, mat_b = _rope_matrix(SWA_DH), _rope_matrix(DIFF_DQ)

    def tables(pos):
        return jnp.concatenate([
            _rope_tables(pos, SWA_DH, g_q_swa[0], g_k_swa[0], SWA_DH ** -0.5 * LOG2E),
            _rope_tables(pos, DIFF_DQ, g_q_diff[0], g_k_diff[0], DIFF_DQ ** -0.5 * LOG2E)])

    tab_tok = tables(jnp.arange(N_META, N_META + seq))
    tab_meta = tables(jnp.arange(N_META))
    g_attn2 = g_attn[0].reshape(1, d).astype(F32)

    x2 = x.reshape(t, d)
    tm = _pick_tile(seq, 512)
    proj = _inproj(x2, g_attn2, w_wide, tab_tok, mat_a, mat_b, tm).reshape(b, seq, PROJ_W)
    proj_meta = _inproj(meta_tokens.astype(F32), g_attn2, w_wide, tab_meta, mat_a, mat_b, N_META)

    def score_bound(dh, gq, gk):
        return dh ** 0.5 * LOG2E * jnp.max(jnp.abs(gq.astype(F32))) * jnp.max(jnp.abs(gk.astype(F32)))

    bound_a = score_bound(SWA_DH, g_q_swa[0], g_k_swa[0])
    swa_scalars = jnp.concatenate([sink_swa[0].astype(F32) * LOG2E, bound_a[None]])
    mixed_a = lax.cond(
        bound_a <= MAX_SHIFT_BOUND,
        lambda: _swa(proj, proj_meta, swa_scalars, 256, 512, True),
        lambda: _swa(proj, proj_meta, swa_scalars, 256, 512, False))
    lam = (jnp.exp(jnp.sum(lambda_q1[0].astype(F32) * lambda_k1[0].astype(F32)))
           - jnp.exp(jnp.sum(lambda_q2[0].astype(F32) * lambda_k2[0].astype(F32))) + lambda_init)
    gh = (jnp.tile(g_diff_head[0].astype(F32), 2) * (1.0 - lambda_init)).reshape(1, LANES)
    bound_b = score_bound(DIFF_DQ, g_q_diff[0], g_k_diff[0])
    scalars = jnp.stack([lam, bound_b])
    mixed_b = lax.cond(
        bound_b <= MAX_SHIFT_BOUND,
        lambda: _diff(proj, proj_meta, scalars, gh, _pick_tile(seq, 512), _pick_tile(seq, 512), True),
        lambda: _diff(proj, proj_meta, scalars, gh, _pick_tile(seq, 512), _pick_tile(seq, 512), False))

    wr = w_router[0].T.astype(F32)
    wrh = wr.astype(BF16)
    wrl = (wr - wrh.astype(F32)).astype(BF16)
    tri = jnp.asarray(np.triu(np.ones((tm, tm), np.float32), 1), BF16)
    h1, v, idx_t, gate_t, rank_t, cnt, runs = _outproj(
        mixed_a.reshape(t, -1), mixed_b.reshape(t, -1), x2, w_out[0].astype(BF16),
        g_ffn[0].reshape(1, d).astype(F32), wrh, wrl, b_router[0].reshape(N_EXPERTS, 1).astype(F32),
        tri, tm, 0, t)

    tme = 512
    experts = jnp.arange(N_EXPERTS, dtype=jnp.int32)

    def per_assignment(table_te):
        tab = jnp.repeat(table_te, COMBINE_TOKENS, axis=0).T
        return jnp.sum(jnp.where(idx_t[None] == experts[:, None, None], tab[:, None, :], 0), axis=0)

    counts = cnt[:, 0].astype(jnp.int32)
    padded = (counts + tme - 1) // tme * tme
    pad_end = jnp.cumsum(padded)
    pad_start = pad_end - padded
    before = runs[:, :, 0].astype(jnp.int32)
    dest_flat = (per_assignment(jnp.broadcast_to(pad_start, before.shape)) + rank_t).T.reshape(-1)
    n_tiles = -(-(TOP_K * t) // tme) + N_EXPERTS
    n_slots = n_tiles * tme
    tiles = jnp.arange(n_tiles, dtype=jnp.int32)
    n_valid = pad_end[-1] // tme
    tile_src = jnp.minimum(tiles, n_valid - 1)
    tile_expert = jnp.minimum(jnp.sum(pad_end[None, :] <= (tile_src * tme)[:, None], axis=1),
                              N_EXPERTS - 1).astype(jnp.int32)
    tile_valid = (tiles < n_valid).astype(jnp.int32)
    in_tile = jnp.concatenate([before[1:], counts[None]]) - before
    run_start = pad_start[None] + before
    lead = run_start % RUN_CHUNK
    n_chunks = jnp.where(in_tile > 0, (lead + in_tile + RUN_CHUNK - 1) // RUN_CHUNK, 0)
    buf_row = RUN_CHUNK * (jnp.cumsum(n_chunks, axis=1) - n_chunks)
    meta_flat = jnp.concatenate([run_start - lead, n_chunks, buf_row, jnp.zeros_like(before)],
                                axis=1).reshape(-1)
    pos_tk = (per_assignment(buf_row + lead - before) + rank_t).T

    fill_meta = jnp.concatenate([pad_start + counts, pad_end, n_valid[None]]).astype(jnp.int32)
    xs = _dispatch(fill_meta, dest_flat, v, n_slots, _pick_tile(t, 2048), tme)
    ys = _experts(tile_expert, tile_src, tile_valid, xs,
                  w_mlp1[0], b_mlp1[0].reshape(N_EXPERTS, 1, -1).astype(F32),
                  w_mlp2[0], b_mlp2[0].reshape(N_EXPERTS, 1, -1).astype(F32), tme)
    out = _combine(meta_flat, pos_tk, gate_t.T, h1, ys)
    return out.reshape(b, seq, d)
```

```python
import functools
import math

import numpy as np
import jax
import jax.numpy as jnp
from jax import lax
from jax.experimental import pallas as pl
from jax.experimental.pallas import tpu as pltpu

F32 = jnp.float32
BF16 = jnp.bfloat16

N_META = 16
WINDOW = 128
ROPE_THETA = 10000.0
EPS = 1e-6
NEG_INF = -1e30
SWA_HQ, SWA_HKV, SWA_DH = 8, 2, 64
DIFF_H, DIFF_DQ, DIFF_DV = 8, 32, 64
N_EXPERTS = 32
TOP_K = 4
SWIGLU_LIMIT = 7.0
SWIGLU_ALPHA = 1.702
LOG2E = math.log2(math.e)
LANES = 128
SUBLANES = 8
VMEM_LIMIT = 56 * 1024 * 1024
MAX_SHIFT_BOUND = 60.0

EXPERT_TILE = 512
FF_CHUNK = 512
ZERO_ROWS = 64
COMBINE_TOKENS = 256
RUN_CHUNK = SUBLANES
COMBINE_ROWS = COMBINE_TOKENS * TOP_K + N_EXPERTS * 2 * RUN_CHUNK
COMBINE_CHUNKS = COMBINE_ROWS // RUN_CHUNK
CHUNK_LIST = 256
CHUNK_GROUP = 8

G_QA, G_KA, G_VA, G_QB, G_KB, G_VB = 0, 4, 6, 8, 12, 16
N_GROUPS = 20
PROJ_W = N_GROUPS * LANES


def _nt_dot(a, b):
    return lax.dot_general(a, b, (((1,), (1,)), ((), ())), preferred_element_type=F32)


def _cparams(sem):
    return pltpu.CompilerParams(dimension_semantics=sem, vmem_limit_bytes=VMEM_LIMIT)


def _inproj_kernel(x_ref, g_ref, w_ref, tab_ref, ma_ref, mb_ref, o_ref):
    x = x_ref[...]
    ms = jnp.mean(x * x, axis=-1, keepdims=True)
    u = (x * lax.rsqrt(ms + EPS) * g_ref[...]).astype(BF16)

    def rope_group(y, mat, inv_dh, c, s):
        z = jnp.concatenate([y.astype(BF16), (y * y).astype(BF16)], axis=1)
        r = jnp.dot(z, mat, preferred_element_type=F32)
        sw, ss = r[:, :LANES], r[:, LANES:]
        return lax.rsqrt(ss * inv_dh + EPS) * (y * c + sw * s)

    chunk = 4 * LANES
    for c0 in range(0, N_GROUPS, 4):
        y4 = jnp.dot(u, w_ref[:, c0 * LANES:c0 * LANES + chunk], preferred_element_type=F32)
        for j in range(4):
            grp = c0 + j
            y = y4[:, j * LANES:(j + 1) * LANES]
            if G_QA <= grp < G_KA:
                y = rope_group(y, ma_ref[...], 1.0 / SWA_DH, tab_ref[0], tab_ref[1])
            elif G_KA <= grp < G_VA:
                y = rope_group(y, ma_ref[...], 1.0 / SWA_DH, tab_ref[2], tab_ref[3])
            elif G_QB <= grp < G_KB:
                y = rope_group(y, mb_ref[...], 1.0 / DIFF_DQ, tab_ref[4], tab_ref[5])
            elif G_KB <= grp < G_VB:
                y = rope_group(y, mb_ref[...], 1.0 / DIFF_DQ, tab_ref[6], tab_ref[7])
            o_ref[:, grp * LANES:(grp + 1) * LANES] = y.astype(BF16)


def _inproj(x2, g, w, tab, mat_a, mat_b, tm):
    rows, d = x2.shape
    n_tab = tab.shape[1] // tm
    return pl.pallas_call(
        _inproj_kernel,
        grid=(rows // tm,),
        in_specs=[
            pl.BlockSpec((tm, d), lambda i: (i, 0)),
            pl.BlockSpec((1, d), lambda i: (0, 0)),
            pl.BlockSpec((d, PROJ_W), lambda i: (0, 0)),
            pl.BlockSpec((8, tm, LANES), lambda i: (0, i % n_tab, 0)),
            pl.BlockSpec((2 * LANES, 2 * LANES), lambda i: (0, 0)),
            pl.BlockSpec((2 * LANES, 2 * LANES), lambda i: (0, 0)),
        ],
        out_specs=pl.BlockSpec((tm, PROJ_W), lambda i: (i, 0)),
        out_shape=jax.ShapeDtypeStruct((rows, PROJ_W), BF16),
        compiler_params=_cparams(("parallel",)), name="inproj",
    )(x2, g, w, tab, mat_a, mat_b)


def _swa_window(tq, win, seq):
    t0 = pl.program_id(1) * tq
    start = pl.multiple_of(jnp.clip(t0 - WINDOW, 0, seq - win), LANES)
    qpos = t0 + lax.broadcasted_iota(jnp.int32, (tq, win), 0)
    kpos = start + lax.broadcasted_iota(jnp.int32, (tq, win), 1)
    return start, jnp.abs(qpos - kpos) <= WINDOW


def _swa_bounded_kernel(sc_ref, q_ref, k_ref, v_ref, km_ref, vm_ref, o_ref, *, tq, win, seq):
    start, ok = _swa_window(tq, win, seq)
    bound = sc_ref[SWA_HQ]
    low_half = lax.broadcasted_iota(jnp.int32, (tq, LANES), 1) < SWA_DH
    n_rep = SWA_HQ // SWA_HKV
    for hk in range(SWA_HKV):
        cols = slice(hk * LANES, (hk + 1) * LANES)
        qs = []
        for pair in range(2):
            qp = q_ref[:, (2 * hk + pair) * LANES:(2 * hk + pair + 1) * LANES]
            qs += [jnp.where(low_half, qp, jnp.zeros_like(qp)), jnp.where(low_half, jnp.zeros_like(qp), qp)]
        q4 = jnp.concatenate(qs, axis=0)
        s = _nt_dot(q4, k_ref[pl.ds(start, win), cols])
        sm = _nt_dot(q4, km_ref[:, cols])
        ps, pms, sink_p = [], [], []
        for r in range(n_rep):
            sink = sc_ref[n_rep * hk + r]
            shift = jnp.maximum(bound, sink)
            rows = slice(r * tq, (r + 1) * tq)
            ps.append(jnp.exp2(jnp.where(ok, s[rows] - shift, NEG_INF)).astype(BF16))
            pms.append(jnp.exp2(sm[rows] - shift).astype(BF16))
            sink_p.append(jnp.exp2(jnp.full((tq, 1), sink - shift, F32)))
        acc = (jnp.dot(jnp.concatenate(ps, axis=0), _with_ones(v_ref[pl.ds(start, win), cols]),
                       preferred_element_type=F32)
               + jnp.dot(jnp.concatenate(pms, axis=0), _with_ones(vm_ref[:, cols]),
                         preferred_element_type=F32))
        outs = [acc[r * tq:(r + 1) * tq, :LANES] / (acc[r * tq:(r + 1) * tq, LANES:LANES + 1] + sink_p[r])
                for r in range(n_rep)]
        for pair in range(2):
            gq = 2 * hk + pair
            o_ref[:, gq * LANES:(gq + 1) * LANES] = jnp.where(
                low_half, outs[2 * pair], outs[2 * pair + 1]).astype(BF16)


def _swa_online_kernel(sink_ref, q_ref, k_ref, v_ref, km_ref, vm_ref, o_ref, *, tq, win, seq):
    start, ok = _swa_window(tq, win, seq)
    low_half = lax.broadcasted_iota(jnp.int32, (tq, LANES), 1) < SWA_DH
    for hk in range(SWA_HKV):
        cols = slice(hk * LANES, (hk + 1) * LANES)
        kw = k_ref[pl.ds(start, win), cols]
        vw = v_ref[pl.ds(start, win), cols]
        km = km_ref[:, cols]
        vm = vm_ref[:, cols]
        for pair in range(2):
            gq = 2 * hk + pair
            qp = q_ref[:, gq * LANES:(gq + 1) * LANES]
            outs = []
            for half in range(2):
                sink = sink_ref[2 * gq + half]
                qm = jnp.where(low_half == (half == 0), qp, jnp.zeros_like(qp))
                s = jnp.where(ok, _nt_dot(qm, kw), NEG_INF)
                sm = _nt_dot(qm, km)
                m = jnp.maximum(jnp.maximum(jnp.max(s, axis=-1, keepdims=True),
                                            jnp.max(sm, axis=-1, keepdims=True)), sink)
                p = jnp.exp2(s - m)
                pm = jnp.exp2(sm - m)
                l = (jnp.sum(p, axis=-1, keepdims=True) + jnp.sum(pm, axis=-1, keepdims=True)
                     + jnp.exp2(sink - m))
                o = (jnp.dot(p.astype(BF16), vw, preferred_element_type=F32)
                     + jnp.dot(pm.astype(BF16), vm, preferred_element_type=F32))
                outs.append(o / l)
            o_ref[:, gq * LANES:(gq + 1) * LANES] = jnp.where(low_half, outs[0], outs[1]).astype(BF16)


def _swa(proj, proj_meta, sink2, tq, win, bounded):
    b, seq, _ = proj.shape
    kern = functools.partial(_swa_bounded_kernel if bounded else _swa_online_kernel, tq=tq, win=win, seq=seq)
    return pl.pallas_call(
        kern,
        grid=(b, seq // tq),
        in_specs=[
            pl.BlockSpec(memory_space=pltpu.SMEM),
            pl.BlockSpec((None, tq, 4 * LANES), lambda bi, i: (bi, i, G_QA // 4)),
            pl.BlockSpec((None, seq, 2 * LANES), lambda bi, i: (bi, 0, G_KA // 2)),
            pl.BlockSpec((None, seq, 2 * LANES), lambda bi, i: (bi, 0, G_VA // 2)),
            pl.BlockSpec((N_META, 2 * LANES), lambda bi, i: (0, G_KA // 2)),
            pl.BlockSpec((N_META, 2 * LANES), lambda bi, i: (0, G_VA // 2)),
        ],
        out_specs=pl.BlockSpec((None, tq, 4 * LANES), lambda bi, i: (bi, i, 0)),
        out_shape=jax.ShapeDtypeStruct((b, seq, 4 * LANES), BF16),
        compiler_params=_cparams(("parallel", "parallel")),
        name="swa_bounded" if bounded else "swa_online",
    )(sink2, proj, proj, proj, proj_meta, proj_meta)


def _quarter_masked(q):
    lane = lax.broadcasted_iota(jnp.int32, q.shape, 1)
    return [jnp.where((lane >= DIFF_DQ * c) & (lane < DIFF_DQ * (c + 1)), q, jnp.zeros_like(q))
            for c in range(4)]


def _with_ones(v):
    return jnp.concatenate([v, jnp.ones_like(v)], axis=1)


def _diff_finish(accs, lam, gh, o_ref):
    def normalized(a):
        return a[:, :LANES] / a[:, LANES:LANES + 1]

    low = lax.broadcasted_iota(jnp.int32, (accs[0].shape[0], LANES), 1) < DIFF_DV
    o = jnp.where(low, normalized(accs[0]) - lam * normalized(accs[1]),
                  normalized(accs[2]) - lam * normalized(accs[3]))
    o2 = o * o
    ss_lo = jnp.sum(jnp.where(low, o2, 0.0), axis=-1, keepdims=True)
    ss_hi = jnp.sum(jnp.where(low, 0.0, o2), axis=-1, keepdims=True)
    rs = lax.rsqrt(jnp.where(low, ss_lo, ss_hi) * (1.0 / DIFF_DV) + 1e-5)
    o_ref[...] = (o * rs * gh).astype(BF16)


def _diff_bounded_kernel(sc_ref, q_ref, k_ref, v_ref, km_ref, vm_ref, gh_ref, o_ref, *, tq, tk, seq):
    bound = sc_ref[1]
    q4 = jnp.concatenate(_quarter_masked(q_ref[...]), axis=0)
    pm = jnp.exp2(_nt_dot(q4, km_ref[...]) - bound).astype(BF16)
    acc = jnp.dot(pm, _with_ones(vm_ref[...]), preferred_element_type=F32)
    for j in range(seq // tk):
        p = jnp.exp2(_nt_dot(q4, k_ref[j * tk:(j + 1) * tk, :]) - bound).astype(BF16)
        acc = acc + jnp.dot(p, _with_ones(v_ref[j * tk:(j + 1) * tk, :]), preferred_element_type=F32)
    _diff_finish([acc[c * tq:(c + 1) * tq] for c in range(4)], sc_ref[0], gh_ref[...], o_ref)


def _diff_online_kernel(sc_ref, q_ref, k_ref, v_ref, km_ref, vm_ref, gh_ref, o_ref, m_ref, acc_ref,
                        *, tq, tk, seq):
    qms = _quarter_masked(q_ref[...])
    km = km_ref[...]
    vm = _with_ones(vm_ref[...])
    for c in range(4):
        s = _nt_dot(qms[c], km)
        m = jnp.max(s, axis=-1, keepdims=True)
        p = jnp.exp2(s - m)
        m_ref[c] = m
        acc_ref[c] = jnp.dot(p.astype(BF16), vm, preferred_element_type=F32)

    def body(j, carry):
        off = pl.multiple_of(j * tk, tk)
        kt = k_ref[pl.ds(off, tk), :]
        vt = _with_ones(v_ref[pl.ds(off, tk), :])
        for c in range(4):
            s = _nt_dot(qms[c], kt)
            m_old = m_ref[c]
            m_new = jnp.maximum(m_old, jnp.max(s, axis=-1, keepdims=True))
            p = jnp.exp2(s - m_new)
            acc_ref[c] = (jnp.exp2(m_old - m_new) * acc_ref[c]
                          + jnp.dot(p.astype(BF16), vt, preferred_element_type=F32))
            m_ref[c] = m_new
        return carry

    lax.fori_loop(0, seq // tk, body, 0)
    _diff_finish([acc_ref[c] for c in range(4)], sc_ref[0], gh_ref[...], o_ref)


def _diff(proj, proj_meta, scalars, gh, tq, tk, bounded):
    b, seq, _ = proj.shape
    n_pairs = DIFF_H // 2
    if bounded:
        kern = functools.partial(_diff_bounded_kernel, tq=tq, tk=tk, seq=seq)
        scratch = []
    else:
        kern = functools.partial(_diff_online_kernel, tq=tq, tk=tk, seq=seq)
        scratch = [pltpu.VMEM((4, tq, 1), F32), pltpu.VMEM((4, tq, 2 * LANES), F32)]
    return pl.pallas_call(
        kern,
        grid=(b, n_pairs, seq // tq),
        in_specs=[
            pl.BlockSpec(memory_space=pltpu.SMEM),
            pl.BlockSpec((None, tq, LANES), lambda bi, hp, i: (bi, i, G_QB + hp)),
            pl.BlockSpec((None, seq, LANES), lambda bi, hp, i: (bi, 0, G_KB + hp)),
            pl.BlockSpec((None, seq, LANES), lambda bi, hp, i: (bi, 0, G_VB + hp)),
            pl.BlockSpec((N_META, LANES), lambda bi, hp, i: (0, G_KB + hp)),
            pl.BlockSpec((N_META, LANES), lambda bi, hp, i: (0, G_VB + hp)),
            pl.BlockSpec((1, LANES), lambda bi, hp, i: (0, 0)),
        ],
        out_specs=pl.BlockSpec((None, tq, LANES), lambda bi, hp, i: (bi, i, hp)),
        out_shape=jax.ShapeDtypeStruct((b, seq, n_pairs * LANES), BF16),
        scratch_shapes=scratch,
        compiler_params=_cparams(("parallel", "parallel", "parallel")),
        name="diff_bounded" if bounded else "diff_online",
    )(scalars, proj, proj, proj, proj_meta, proj_meta, gh)


def _outproj_kernel(ma_ref, mb_ref, x_ref, wo_ref, g_ref, wrh_ref, wrl_ref, br_ref, tri_ref,
                    h1_ref, v_ref, idx_ref, gate_ref, rank_ref, cnt_ref, run_ref, carry_ref, *, tm):
    half = wo_ref.shape[0] // 2

    @pl.when(pl.program_id(0) == 0)
    def _():
        carry_ref[...] = jnp.zeros_like(carry_ref)

    h1 = (x_ref[...]
          + jnp.dot(ma_ref[...], wo_ref[:half, :], preferred_element_type=F32)
          + jnp.dot(mb_ref[...], wo_ref[half:, :], preferred_element_type=F32))
    h1_ref[...] = h1
    v = h1 * lax.rsqrt(jnp.mean(h1 * h1, axis=-1, keepdims=True) + EPS) * g_ref[...]
    v_ref[...] = v
    vh = v.astype(BF16)
    vl = (v - vh.astype(F32)).astype(BF16)
    work = (_nt_dot(wrh_ref[...], vh) + _nt_dot(wrl_ref[...], vh) + _nt_dot(wrh_ref[...], vl)
            + br_ref[...])
    iota_e = lax.broadcasted_iota(jnp.int32, (N_EXPERTS, tm), 0)
    vals, idxs, sels = [], [], []
    for _ in range(TOP_K):
        mk = jnp.max(work, axis=0, keepdims=True)
        ik = jnp.min(jnp.where(work == mk, iota_e, N_EXPERTS), axis=0, keepdims=True)
        sel = iota_e == ik
        work = jnp.where(sel, -jnp.inf, work)
        vals.append(mk)
        idxs.append(ik)
        sels.append(sel)
    exps = [jnp.exp(vk - vals[0]) for vk in vals]
    denom = exps[0] + exps[1] + exps[2] + exps[3]
    gate_ref[...] = jnp.concatenate([e / denom for e in exps], axis=0)
    idx_ref[...] = jnp.concatenate(idxs, axis=0)
    cnt = jnp.zeros((N_EXPERTS, tm), F32)
    for sel in sels:
        cnt = cnt + jnp.where(sel, 1.0, 0.0)
    before = jnp.dot(cnt.astype(BF16), tri_ref[...], preferred_element_type=F32) + carry_ref[...]
    ranks = [jnp.sum(jnp.where(sel, before, 0.0), axis=0, keepdims=True) for sel in sels]
    rank_ref[...] = jnp.concatenate(ranks, axis=0).astype(jnp.int32)
    run = carry_ref[...]
    for j in range(run_ref.shape[0]):
        run_ref[j] = jnp.broadcast_to(run, run_ref.shape[1:])
        run = run + jnp.sum(cnt[:, j * COMBINE_TOKENS:(j + 1) * COMBINE_TOKENS], axis=1, keepdims=True)
    carry_ref[...] = run
    cnt_ref[...] = jnp.broadcast_to(run, cnt_ref.shape)


def _outproj(mixed_a, mixed_b, x2, wo, g, wrh, wrl, br, tri, tm, row0, t):
    d = x2.shape[1]
    hw = mixed_a.shape[1]
    kern = functools.partial(_outproj_kernel, tm=tm)
    blk0 = row0 // tm
    src = lambda i: (i + blk0, 0)
    row = lambda i: (i, 0)
    fix = lambda i: (0, 0)
    col = lambda i: (0, i)
    return pl.pallas_call(
        kern,
        grid=(t // tm,),
        in_specs=[
            pl.BlockSpec((tm, hw), src), pl.BlockSpec((tm, hw), src), pl.BlockSpec((tm, d), src),
            pl.BlockSpec((d, d), fix), pl.BlockSpec((1, d), fix),
            pl.BlockSpec((N_EXPERTS, d), fix), pl.BlockSpec((N_EXPERTS, d), fix),
            pl.BlockSpec((N_EXPERTS, 1), fix), pl.BlockSpec((tm, tm), fix),
        ],
        out_specs=[
            pl.BlockSpec((tm, d), row), pl.BlockSpec((tm, d), row),
            pl.BlockSpec((TOP_K, tm), col), pl.BlockSpec((TOP_K, tm), col),
            pl.BlockSpec((TOP_K, tm), col), pl.BlockSpec((N_EXPERTS, LANES), fix),
            pl.BlockSpec((tm // COMBINE_TOKENS, N_EXPERTS, LANES), lambda i: (i, 0, 0)),
        ],
        out_shape=[
            jax.ShapeDtypeStruct((t, d), F32), jax.ShapeDtypeStruct((t, d), F32),
            jax.ShapeDtypeStruct((TOP_K, t), jnp.int32), jax.ShapeDtypeStruct((TOP_K, t), F32),
            jax.ShapeDtypeStruct((TOP_K, t), jnp.int32),
            jax.ShapeDtypeStruct((N_EXPERTS, LANES), F32),
            jax.ShapeDtypeStruct((t // COMBINE_TOKENS, N_EXPERTS, LANES), F32),
        ],
        scratch_shapes=[pltpu.VMEM((N_EXPERTS, 1), F32)],
        compiler_params=_cparams(("arbitrary",)), name="outproj",
    )(mixed_a, mixed_b, x2, wo, g, wrh, wrl, br, tri)


def _zero_fill(fill_ref, xs_hbm, zero_ref, zsem, tme):
    n_tiles = xs_hbm.shape[0] // tme
    zero_ref[...] = jnp.zeros_like(zero_ref)

    def row_copy(r):
        return pltpu.make_async_copy(zero_ref.at[pl.ds(0, 1)], xs_hbm.at[pl.ds(r, 1)], zsem)

    def chunk_copy(c):
        return pltpu.make_async_copy(zero_ref, xs_hbm.at[pl.ds(c * ZERO_ROWS, ZERO_ROWS)], zsem)

    def fill(start):
        for e in range(N_EXPERTS):
            def one_row(r, carry):
                row_copy(r).start() if start else row_copy(r).wait()
                return carry
            lax.fori_loop(fill_ref[e], fill_ref[N_EXPERTS + e], one_row, 0)

        def one_chunk(c, carry):
            chunk_copy(c).start() if start else chunk_copy(c).wait()
            return carry
        per_tile = tme // ZERO_ROWS
        lax.fori_loop(fill_ref[2 * N_EXPERTS] * per_tile, n_tiles * per_tile, one_chunk, 0)

    fill(True)
    fill(False)


def _issue_rows(dest_ref, v_ref, xs_hbm, sem, t):
    for k in range(TOP_K):
        pltpu.make_async_copy(v_ref.at[pl.ds(t, 1)],
                              xs_hbm.at[pl.ds(dest_ref[TOP_K * t + k], 1)], sem).start(priority=k % 2)


def _wait_rows(v_ref, xs_hbm, sem):
    for _ in range(TOP_K):
        pltpu.make_async_copy(v_ref, xs_hbm.at[pl.ds(0, v_ref.shape[0])], sem).wait()


def _dispatch_kernel(fill_ref, dest_ref, v_ref, xs_hbm, zero_ref, sem, zsem, *, tmd, tme):
    @pl.when(pl.program_id(0) == 0)
    def _():
        _zero_fill(fill_ref, xs_hbm, zero_ref, zsem, tme)

    def issue(t, carry):
        _issue_rows(dest_ref, v_ref, xs_hbm, sem, t)
        return carry

    lax.fori_loop(0, tmd, issue, 0, unroll=8)
    _wait_rows(v_ref, xs_hbm, sem)


def _dispatch(fill_meta, dest_flat, v, n_slots, tmd, tme):
    t, d = v.shape
    kern = functools.partial(_dispatch_kernel, tmd=tmd, tme=tme)
    grid_spec = pltpu.PrefetchScalarGridSpec(
        num_scalar_prefetch=1,
        grid=(t // tmd,),
        in_specs=[
            pl.BlockSpec((TOP_K * tmd,), lambda i, fm: (i,), memory_space=pltpu.SMEM),
            pl.BlockSpec((tmd, d), lambda i, fm: (i, 0)),
        ],
        out_specs=pl.BlockSpec(memory_space=pl.ANY),
        scratch_shapes=[pltpu.VMEM((ZERO_ROWS, d), F32), pltpu.SemaphoreType.DMA, pltpu.SemaphoreType.DMA],
    )
    return pl.pallas_call(
        kern,
        grid_spec=grid_spec,
        out_shape=jax.ShapeDtypeStruct((n_slots, d), F32),
        compiler_params=_cparams(("arbitrary",)), name="dispatch",
    )(fill_meta, dest_flat, v)


def _expert_mlp(xs_ref, w1_ref, b1_ref, w2_ref, b2_ref, ys_ref):
    d_ff = w2_ref.shape[0]
    x = xs_ref[...]
    y = jnp.broadcast_to(b2_ref[...], ys_ref.shape)
    for c0 in range(0, d_ff, FF_CHUNK):
        cg, cl = slice(c0, c0 + FF_CHUNK), slice(d_ff + c0, d_ff + c0 + FF_CHUNK)
        glu = jnp.dot(x, w1_ref[:, cg], preferred_element_type=F32) + b1_ref[:, cg]
        lin = jnp.dot(x, w1_ref[:, cl], preferred_element_type=F32) + b1_ref[:, cl]
        glu = jnp.minimum(glu, SWIGLU_LIMIT)
        lin = jnp.clip(lin, -SWIGLU_LIMIT, SWIGLU_LIMIT)
        act = glu * jax.nn.sigmoid(SWIGLU_ALPHA * glu) * (lin + 1.0)
        y = y + jnp.dot(act, w2_ref[cg, :], preferred_element_type=F32)
    ys_ref[...] = y


def _experts_kernel(te_ref, ts_ref, tv_ref, xs_ref, w1_ref, b1_ref, w2_ref, b2_ref, ys_ref):
    del te_ref, ts_ref

    @pl.when(tv_ref[pl.program_id(0)] > 0)
    def _():
        _expert_mlp(xs_ref, w1_ref, b1_ref, w2_ref, b2_ref, ys_ref)

    @pl.when(tv_ref[pl.program_id(0)] == 0)
    def _():
        ys_ref[...] = jnp.zeros_like(ys_ref)


def _experts(tile_expert, tile_src, tile_valid, xs, w1, b1, w2, b2, tm):
    n_slots, d = xs.shape
    two_ff = w1.shape[2]
    d_ff = two_ff // 2
    grid_spec = pltpu.PrefetchScalarGridSpec(
        num_scalar_prefetch=3,
        grid=(n_slots // tm,),
        in_specs=[
            pl.BlockSpec((tm, d), lambda i, te, ts, tv: (ts[i], 0)),
            pl.BlockSpec((None, d, two_ff), lambda i, te, ts, tv: (te[i], 0, 0)),
            pl.BlockSpec((None, 1, two_ff), lambda i, te, ts, tv: (te[i], 0, 0)),
            pl.BlockSpec((None, d_ff, d), lambda i, te, ts, tv: (te[i], 0, 0)),
            pl.BlockSpec((None, 1, d), lambda i, te, ts, tv: (te[i], 0, 0)),
        ],
        out_specs=pl.BlockSpec((tm, d), lambda i, te, ts, tv: (i, 0)),
    )
    return pl.pallas_call(
        _experts_kernel, grid_spec=grid_spec, out_shape=jax.ShapeDtypeStruct(xs.shape, F32),
        compiler_params=_cparams(("arbitrary",)), name="experts",
    )(tile_expert, tile_src, tile_valid, xs, w1, b1, w2, b2)


def _combine_kernel(src_ref, src_next_ref, pos_ref, gate_ref, h1_ref, ys_hbm, o_ref, buf_ref, sem):
    i = pl.program_id(0)
    n = pl.num_programs(0)
    group_rows = CHUNK_GROUP * RUN_CHUNK

    def issue(s_ref, slot):
        def one_group(g, carry):
            for u in range(CHUNK_GROUP):
                q = g * CHUNK_GROUP + u
                src = pl.multiple_of(s_ref[q], RUN_CHUNK)
                dst = pl.multiple_of(q * RUN_CHUNK, RUN_CHUNK)
                pltpu.make_async_copy(ys_hbm.at[pl.ds(src, RUN_CHUNK)],
                                      buf_ref.at[slot, pl.ds(dst, RUN_CHUNK)],
                                      sem.at[slot]).start(priority=u % 2)
            return carry
        lax.fori_loop(0, s_ref[CHUNK_LIST - 1], one_group, 0)

    @pl.when(i == 0)
    def _():
        buf_ref[...] = jnp.zeros_like(buf_ref)
        issue(src_ref, 0)

    slot = i % 2

    @pl.when(i + 1 < n)
    def _():
        issue(src_next_ref, 1 - slot)

    def wait_group(g, carry):
        pltpu.make_async_copy(buf_ref.at[slot, pl.ds(0, group_rows)],
                              buf_ref.at[slot, pl.ds(0, group_rows)], sem.at[slot]).wait()
        return carry
    lax.fori_loop(0, src_ref[CHUNK_LIST - 1], wait_group, 0)
    rows = buf_ref[slot].astype(BF16)
    col = lax.broadcasted_iota(jnp.int32, (pos_ref.shape[0], COMBINE_ROWS), 1)
    g = jnp.zeros(col.shape, F32)
    for k in reversed(range(TOP_K)):
        g = jnp.where(col == pos_ref[:, k:k + 1], gate_ref[:, k:k + 1], g)
    g_hi = g.astype(BF16)
    g_lo = (g - g_hi.astype(F32)).astype(BF16)
    y2 = jnp.dot(jnp.concatenate([g_hi, g_lo], axis=0), rows, preferred_element_type=F32)
    n_tok = pos_ref.shape[0]
    o_ref[...] = h1_ref[...] + y2[:n_tok] + y2[n_tok:]


def _combine(chunk_src, pos_tk, gate_tk, h1, ys):
    t, d = h1.shape
    tmc = COMBINE_TOKENS
    n = t // tmc
    return pl.pallas_call(
        _combine_kernel,
        grid=(n,),
        in_specs=[
            pl.BlockSpec((CHUNK_LIST,), lambda i: (i,), memory_space=pltpu.SMEM),
            pl.BlockSpec((CHUNK_LIST,), lambda i: (jnp.minimum(i + 1, n - 1),), memory_space=pltpu.SMEM),
            pl.BlockSpec((tmc, TOP_K), lambda i: (i, 0)),
            pl.BlockSpec((tmc, TOP_K), lambda i: (i, 0)),
            pl.BlockSpec((tmc, d), lambda i: (i, 0)),
            pl.BlockSpec(memory_space=pl.ANY),
        ],
        out_specs=pl.BlockSpec((tmc, d), lambda i: (i, 0)),
        out_shape=jax.ShapeDtypeStruct((t, d), F32),
        scratch_shapes=[
            pltpu.VMEM((2, COMBINE_ROWS, d), F32),
            pltpu.SemaphoreType.DMA((2,)),
        ],
        compiler_params=_cparams(("arbitrary",)), name="combine",
    )(chunk_src, chunk_src, pos_tk, gate_tk, h1, ys)


def _swap_halves(g, dh):
    return jnp.concatenate([g[dh // 2:], g[:dh // 2]])


def _rope_tables(pos, dh, gq, gk, q_scale):
    inv = ROPE_THETA ** (-jnp.arange(0, dh, 2, dtype=F32) / dh)
    ang = pos.astype(F32)[:, None] * inv[None, :]
    cos = jnp.concatenate([jnp.cos(ang), jnp.cos(ang)], axis=1)
    sin = jnp.concatenate([-jnp.sin(ang), jnp.sin(ang)], axis=1)
    rep = LANES // dh
    tile = lambda a: jnp.tile(a, (1, rep))
    gq, gk = gq.astype(F32), gk.astype(F32)
    return jnp.stack([
        tile(cos * gq[None]) * q_scale, tile(sin * _swap_halves(gq, dh)[None]) * q_scale,
        tile(cos * gk[None]), tile(sin * _swap_halves(gk, dh)[None]),
    ])


def _rope_matrix(dh):
    j = np.arange(LANES)
    swap = (j // dh) * dh + (j % dh + dh // 2) % dh
    m = np.zeros((2 * LANES, 2 * LANES), np.float32)
    m[swap, j] = 1.0
    m[LANES:, LANES:] = (j[:, None] // dh == j[None, :] // dh)
    return jnp.asarray(m, BF16)


def _pick_tile(n, pref):
    t = min(n, pref)
    while n % t:
        t //= 2
    return t


def _tiles(seq, t):
    return dict(
        inproj=_pick_tile(seq, 512),
        swa_q=256, swa_win=512,
        diff_q=_pick_tile(seq, 512),
        diff_k=_pick_tile(seq, 512),
        outproj=_pick_tile(seq, 1024),
        dispatch=_pick_tile(t, 2048),
    )


def kernel(x, meta_tokens, g_attn, w_in, g_q_swa, g_k_swa, g_q_diff, g_k_diff, sink_swa, lambda_q1, lambda_k1, lambda_q2, lambda_k2, g_diff_head, w_out, g_ffn, w_router, b_router, w_mlp1, b_mlp1, w_mlp2, b_mlp2):
    b, seq, d = x.shape
    assert g_attn.shape[0] == 1 and d % LANES == 0 and seq % 512 == 0
    t = b * seq
    lambda_init = 0.8 - 0.6 * math.exp(-0.3 * 0)

    wi = w_in[0]
    q_a, k_a, v_a, q_b, k_b, v_b = (wi[:, s:e] for s, e in
                                    ((0, 512), (512, 640), (640, 768), (768, 1280), (1280, 1792), (1792, 2304)))
    dup = lambda w: jnp.concatenate([w[:, :64], w[:, :64], w[:, 64:], w[:, 64:]], axis=1)
    w_wide = jnp.concatenate([q_a, dup(k_a), dup(v_a), q_b, k_b, v_b], axis=1).astype(BF16)
    mat_a, mat_b = _rope_matrix(SWA_DH), _rope_matrix(DIFF_DQ)

    def tables(pos):
        return jnp.concatenate([
            _rope_tables(pos, SWA_DH, g_q_swa[0], g_k_swa[0], SWA_DH ** -0.5 * LOG2E),
            _rope_tables(pos, DIFF_DQ, g_q_diff[0], g_k_diff[0], DIFF_DQ ** -0.5 * LOG2E)])

    tab_tok = tables(jnp.arange(N_META, N_META + seq))
    tab_meta = tables(jnp.arange(N_META))
    g_attn2 = g_attn[0].reshape(1, d).astype(F32)

    x2 = x.reshape(t, d)
    tiles = _tiles(seq, t)
    proj = _inproj(x2, g_attn2, w_wide, tab_tok, mat_a, mat_b, tiles["inproj"]).reshape(b, seq, PROJ_W)
    proj_meta = _inproj(meta_tokens.astype(F32), g_attn2, w_wide, tab_meta, mat_a, mat_b, N_META)

    def score_bound(dh, gq, gk):
        return dh ** 0.5 * LOG2E * jnp.max(jnp.abs(gq.astype(F32))) * jnp.max(jnp.abs(gk.astype(F32)))

    bound_a = score_bound(SWA_DH, g_q_swa[0], g_k_swa[0])
    swa_scalars = jnp.concatenate([sink_swa[0].astype(F32) * LOG2E, bound_a[None]])
    mixed_a = lax.cond(
        bound_a <= MAX_SHIFT_BOUND,
        lambda: _swa(proj, proj_meta, swa_scalars, tiles["swa_q"], tiles["swa_win"], True),
        lambda: _swa(proj, proj_meta, swa_scalars, tiles["swa_q"], tiles["swa_win"], False))
    lam = (jnp.exp(jnp.sum(lambda_q1[0].astype(F32) * lambda_k1[0].astype(F32)))
           - jnp.exp(jnp.sum(lambda_q2[0].astype(F32) * lambda_k2[0].astype(F32))) + lambda_init)
    gh = (jnp.tile(g_diff_head[0].astype(F32), 2) * (1.0 - lambda_init)).reshape(1, LANES)
    bound_b = score_bound(DIFF_DQ, g_q_diff[0], g_k_diff[0])
    scalars = jnp.stack([lam, bound_b])
    mixed_b = lax.cond(
        bound_b <= MAX_SHIFT_BOUND,
        lambda: _diff(proj, proj_meta, scalars, gh, tiles["diff_q"], tiles["diff_k"], True),
        lambda: _diff(proj, proj_meta, scalars, gh, tiles["diff_q"], tiles["diff_k"], False))

    wr = w_router[0].T.astype(F32)
    wrh = wr.astype(BF16)
    wrl = (wr - wrh.astype(F32)).astype(BF16)
    tmo = tiles["outproj"]
    tri = jnp.asarray(np.triu(np.ones((tmo, tmo), np.float32), 1), BF16)
    h1, v, idx_t, gate_t, rank_t, cnt, runs = _outproj(
        mixed_a.reshape(t, -1), mixed_b.reshape(t, -1), x2, w_out[0].astype(BF16),
        g_ffn[0].reshape(1, d).astype(F32), wrh, wrl, b_router[0].reshape(N_EXPERTS, 1).astype(F32),
        tri, tmo, 0, t)

    tme = EXPERT_TILE
    experts = jnp.arange(N_EXPERTS, dtype=jnp.int32)

    def per_assignment(table_te):
        tab = jnp.repeat(table_te, COMBINE_TOKENS, axis=0).T
        return jnp.sum(jnp.where(idx_t[None] == experts[:, None, None], tab[:, None, :], 0), axis=0)

    counts = cnt[:, 0].astype(jnp.int32)
    padded = (counts + tme - 1) // tme * tme
    pad_end = jnp.cumsum(padded)
    pad_start = pad_end - padded
    before = runs[:, :, 0].astype(jnp.int32)
    dest_flat = (per_assignment(jnp.broadcast_to(pad_start, before.shape)) + rank_t).T.reshape(-1)
    n_tiles = -(-(TOP_K * t) // tme) + N_EXPERTS
    n_slots = n_tiles * tme
    tile_ids = jnp.arange(n_tiles, dtype=jnp.int32)
    n_valid = pad_end[-1] // tme
    tile_src = jnp.minimum(tile_ids, n_valid - 1)
    tile_expert = jnp.minimum(jnp.sum(pad_end[None, :] <= (tile_src * tme)[:, None], axis=1),
                              N_EXPERTS - 1).astype(jnp.int32)
    tile_valid = (tile_ids < n_valid).astype(jnp.int32)
    in_tile = jnp.concatenate([before[1:], counts[None]]) - before
    run_start = pad_start[None] + before
    lead = run_start % RUN_CHUNK
    n_chunks = jnp.where(in_tile > 0, (lead + in_tile + RUN_CHUNK - 1) // RUN_CHUNK, 0)
    chunk_end = jnp.cumsum(n_chunks, axis=1)
    chunk_first = chunk_end - n_chunks
    pos_tk = (per_assignment(RUN_CHUNK * chunk_first + lead - before) + rank_t).T
    q = jnp.arange(CHUNK_LIST, dtype=jnp.int32)
    owner = jnp.sum(chunk_end[:, None, :] <= q[None, :, None], axis=2)
    is_owner = owner[:, :, None] == experts[None, None, :]
    src_of = (run_start - lead - RUN_CHUNK * chunk_first)[:, None, :] + RUN_CHUNK * q[None, :, None]
    chunk_src = jnp.sum(jnp.where(is_owner, src_of, 0), axis=2)
    n_groups = (chunk_end[:, -1] + CHUNK_GROUP - 1) // CHUNK_GROUP
    chunk_src = chunk_src.at[:, CHUNK_LIST - 1].set(n_groups).reshape(-1)

    fill_meta = jnp.concatenate([pad_start + counts, pad_end, n_valid[None]]).astype(jnp.int32)
    xs = _dispatch(fill_meta, dest_flat, v, n_slots, tiles["dispatch"], tme)
    ys = _experts(tile_expert, tile_src, tile_valid, xs,
                  w_mlp1[0], b_mlp1[0].reshape(N_EXPERTS, 1, -1).astype(F32),
                  w_mlp2[0], b_mlp2[0].reshape(N_EXPERTS, 1, -1).astype(F32), tme)
    out = _combine(chunk_src, pos_tk, gate_t.T, h1, ys)
    return out.reshape(b, seq, d)
```

```python
import functools
import math

import numpy as np
import jax
import jax.numpy as jnp
from jax import lax
from jax.experimental import pallas as pl
from jax.experimental.pallas import tpu as pltpu

F32 = jnp.float32
BF16 = jnp.bfloat16

N_META = 16
WINDOW = 128
ROPE_THETA = 10000.0
EPS = 1e-6
NEG_INF = -1e30
SWA_HQ, SWA_HKV, SWA_DH = 8, 2, 64
DIFF_H, DIFF_DQ, DIFF_DV = 8, 32, 64
N_EXPERTS = 32
TOP_K = 4
SWIGLU_LIMIT = 7.0
SWIGLU_ALPHA = 1.702
LOG2E = math.log2(math.e)
LANES = 128
SUBLANES = 8
VMEM_LIMIT = 56 * 1024 * 1024
MAX_SHIFT_BOUND = 60.0

EXPERT_TILE = 512
FF_CHUNK = 512
ZERO_ROWS = 64
COMBINE_TOKENS = 256
RUN_CHUNK = SUBLANES
COMBINE_ROWS = COMBINE_TOKENS * TOP_K + N_EXPERTS * 2 * RUN_CHUNK
COMBINE_CHUNKS = COMBINE_ROWS // RUN_CHUNK
CHUNK_LIST = 256
CHUNK_GROUP = 8

G_QA, G_KA, G_VA, G_QB, G_KB, G_VB = 0, 4, 6, 8, 12, 16
N_GROUPS = 20
PROJ_W = N_GROUPS * LANES


def _nt_dot(a, b):
    return lax.dot_general(a, b, (((1,), (1,)), ((), ())), preferred_element_type=F32)


def _cparams(sem):
    return pltpu.CompilerParams(dimension_semantics=sem, vmem_limit_bytes=VMEM_LIMIT)


def _inproj_kernel(x_ref, g_ref, w_ref, tab_ref, ma_ref, mb_ref, o_ref):
    x = x_ref[...]
    ms = jnp.mean(x * x, axis=-1, keepdims=True)
    u = (x * lax.rsqrt(ms + EPS) * g_ref[...]).astype(BF16)

    def rope_group(y, mat, inv_dh, c, s):
        z = jnp.concatenate([y.astype(BF16), (y * y).astype(BF16)], axis=1)
        r = jnp.dot(z, mat, preferred_element_type=F32)
        sw, ss = r[:, :LANES], r[:, LANES:]
        return lax.rsqrt(ss * inv_dh + EPS) * (y * c + sw * s)

    chunk = 4 * LANES
    for c0 in range(0, N_GROUPS, 4):
        y4 = jnp.dot(u, w_ref[:, c0 * LANES:c0 * LANES + chunk], preferred_element_type=F32)
        for j in range(4):
            grp = c0 + j
            y = y4[:, j * LANES:(j + 1) * LANES]
            if G_QA <= grp < G_KA:
                y = rope_group(y, ma_ref[...], 1.0 / SWA_DH, tab_ref[0], tab_ref[1])
            elif G_KA <= grp < G_VA:
                y = rope_group(y, ma_ref[...], 1.0 / SWA_DH, tab_ref[2], tab_ref[3])
            elif G_QB <= grp < G_KB:
                y = rope_group(y, mb_ref[...], 1.0 / DIFF_DQ, tab_ref[4], tab_ref[5])
            elif G_KB <= grp < G_VB:
                y = rope_group(y, mb_ref[...], 1.0 / DIFF_DQ, tab_ref[6], tab_ref[7])
            o_ref[:, grp * LANES:(grp + 1) * LANES] = y.astype(BF16)


def _inproj(x2, g, w, tab, mat_a, mat_b, tm):
    rows, d = x2.shape
    n_tab = tab.shape[1] // tm
    return pl.pallas_call(
        _inproj_kernel,
        grid=(rows // tm,),
        in_specs=[
            pl.BlockSpec((tm, d), lambda i: (i, 0)),
            pl.BlockSpec((1, d), lambda i: (0, 0)),
            pl.BlockSpec((d, PROJ_W), lambda i: (0, 0)),
            pl.BlockSpec((8, tm, LANES), lambda i: (0, i % n_tab, 0)),
            pl.BlockSpec((2 * LANES, 2 * LANES), lambda i: (0, 0)),
            pl.BlockSpec((2 * LANES, 2 * LANES), lambda i: (0, 0)),
        ],
        out_specs=pl.BlockSpec((tm, PROJ_W), lambda i: (i, 0)),
        out_shape=jax.ShapeDtypeStruct((rows, PROJ_W), BF16),
        compiler_params=_cparams(("parallel",)), name="inproj",
    )(x2, g, w, tab, mat_a, mat_b)


def _swa_window(tq, win, seq):
    t0 = pl.program_id(1) * tq
    start = pl.multiple_of(jnp.clip(t0 - WINDOW, 0, seq - win), LANES)
    qpos = t0 + lax.broadcasted_iota(jnp.int32, (tq, win), 0)
    kpos = start + lax.broadcasted_iota(jnp.int32, (tq, win), 1)
    return start, jnp.abs(qpos - kpos) <= WINDOW


def _swa_bounded_kernel(sc_ref, q_ref, k_ref, v_ref, km_ref, vm_ref, o_ref, *, tq, win, seq):
    start, ok = _swa_window(tq, win, seq)
    bound = sc_ref[SWA_HQ]
    low_half = lax.broadcasted_iota(jnp.int32, (tq, LANES), 1) < SWA_DH
    n_rep = SWA_HQ // SWA_HKV
    for hk in range(SWA_HKV):
        cols = slice(hk * LANES, (hk + 1) * LANES)
        qs = []
        for pair in range(2):
            qp = q_ref[:, (2 * hk + pair) * LANES:(2 * hk + pair + 1) * LANES]
            qs += [jnp.where(low_half, qp, jnp.zeros_like(qp)), jnp.where(low_half, jnp.zeros_like(qp), qp)]
        q4 = jnp.concatenate(qs, axis=0)
        s = _nt_dot(q4, k_ref[pl.ds(start, win), cols])
        sm = _nt_dot(q4, km_ref[:, cols])
        ps, pms, sink_p = [], [], []
        for r in range(n_rep):
            sink = sc_ref[n_rep * hk + r]
            shift = jnp.maximum(bound, sink)
            rows = slice(r * tq, (r + 1) * tq)
            ps.append(jnp.exp2(jnp.where(ok, s[rows] - shift, NEG_INF)).astype(BF16))
            pms.append(jnp.exp2(sm[rows] - shift).astype(BF16))
            sink_p.append(jnp.exp2(jnp.full((tq, 1), sink - shift, F32)))
        acc = (jnp.dot(jnp.concatenate(ps, axis=0), _with_ones(v_ref[pl.ds(start, win), cols]),
                       preferred_element_type=F32)
               + jnp.dot(jnp.concatenate(pms, axis=0), _with_ones(vm_ref[:, cols]),
                         preferred_element_type=F32))
        outs = [acc[r * tq:(r + 1) * tq, :LANES] / (acc[r * tq:(r + 1) * tq, LANES:LANES + 1] + sink_p[r])
                for r in range(n_rep)]
        for pair in range(2):
            gq = 2 * hk + pair
            o_ref[:, gq * LANES:(gq + 1) * LANES] = jnp.where(
                low_half, outs[2 * pair], outs[2 * pair + 1]).astype(BF16)


def _swa_online_kernel(sink_ref, q_ref, k_ref, v_ref, km_ref, vm_ref, o_ref, *, tq, win, seq):
    start, ok = _swa_window(tq, win, seq)
    low_half = lax.broadcasted_iota(jnp.int32, (tq, LANES), 1) < SWA_DH
    for hk in range(SWA_HKV):
        cols = slice(hk * LANES, (hk + 1) * LANES)
        kw = k_ref[pl.ds(start, win), cols]
        vw = v_ref[pl.ds(start, win), cols]
        km = km_ref[:, cols]
        vm = vm_ref[:, cols]
        for pair in range(2):
            gq = 2 * hk + pair
            qp = q_ref[:, gq * LANES:(gq + 1) * LANES]
            outs = []
            for half in range(2):
                sink = sink_ref[2 * gq + half]
                qm = jnp.where(low_half == (half == 0), qp, jnp.zeros_like(qp))
                s = jnp.where(ok, _nt_dot(qm, kw), NEG_INF)
                sm = _nt_dot(qm, km)
                m = jnp.maximum(jnp.maximum(jnp.max(s, axis=-1, keepdims=True),
                                            jnp.max(sm, axis=-1, keepdims=True)), sink)
                p = jnp.exp2(s - m)
                pm = jnp.exp2(sm - m)
                l = (jnp.sum(p, axis=-1, keepdims=True) + jnp.sum(pm, axis=-1, keepdims=True)
                     + jnp.exp2(sink - m))
                o = (jnp.dot(p.astype(BF16), vw, preferred_element_type=F32)
                     + jnp.dot(pm.astype(BF16), vm, preferred_element_type=F32))
                outs.append(o / l)
            o_ref[:, gq * LANES:(gq + 1) * LANES] = jnp.where(low_half, outs[0], outs[1]).astype(BF16)


def _swa(proj, proj_meta, sink2, tq, win, bounded):
    b, seq, _ = proj.shape
    kern = functools.partial(_swa_bounded_kernel if bounded else _swa_online_kernel, tq=tq, win=win, seq=seq)
    return pl.pallas_call(
        kern,
        grid=(b, seq // tq),
        in_specs=[
            pl.BlockSpec(memory_space=pltpu.SMEM),
            pl.BlockSpec((None, tq, 4 * LANES), lambda bi, i: (bi, i, G_QA // 4)),
            pl.BlockSpec((None, seq, 2 * LANES), lambda bi, i: (bi, 0, G_KA // 2)),
            pl.BlockSpec((None, seq, 2 * LANES), lambda bi, i: (bi, 0, G_VA // 2)),
            pl.BlockSpec((N_META, 2 * LANES), lambda bi, i: (0, G_KA // 2)),
            pl.BlockSpec((N_META, 2 * LANES), lambda bi, i: (0, G_VA // 2)),
        ],
        out_specs=pl.BlockSpec((None, tq, 4 * LANES), lambda bi, i: (bi, i, 0)),
        out_shape=jax.ShapeDtypeStruct((b, seq, 4 * LANES), BF16),
        compiler_params=_cparams(("parallel", "parallel")),
        name="swa_bounded" if bounded else "swa_online",
    )(sink2, proj, proj, proj, proj_meta, proj_meta)


def _quarter_masked(q):
    lane = lax.broadcasted_iota(jnp.int32, q.shape, 1)
    return [jnp.where((lane >= DIFF_DQ * c) & (lane < DIFF_DQ * (c + 1)), q, jnp.zeros_like(q))
            for c in range(4)]


def _with_ones(v):
    return jnp.concatenate([v, jnp.ones_like(v)], axis=1)


def _diff_finish(accs, lam, gh, o_ref):
    def normalized(a):
        return a[:, :LANES] / a[:, LANES:LANES + 1]

    low = lax.broadcasted_iota(jnp.int32, (accs[0].shape[0], LANES), 1) < DIFF_DV
    o = jnp.where(low, normalized(accs[0]) - lam * normalized(accs[1]),
                  normalized(accs[2]) - lam * normalized(accs[3]))
    o2 = o * o
    ss_lo = jnp.sum(jnp.where(low, o2, 0.0), axis=-1, keepdims=True)
    ss_hi = jnp.sum(jnp.where(low, 0.0, o2), axis=-1, keepdims=True)
    rs = lax.rsqrt(jnp.where(low, ss_lo, ss_hi) * (1.0 / DIFF_DV) + 1e-5)
    o_ref[...] = (o * rs * gh).astype(BF16)


def _diff_bounded_kernel(sc_ref, q_ref, k_ref, v_ref, km_ref, vm_ref, gh_ref, o_ref, *, tq, tk, seq):
    bound = sc_ref[1]
    q4 = jnp.concatenate(_quarter_masked(q_ref[...]), axis=0)
    pm = jnp.exp2(_nt_dot(q4, km_ref[...]) - bound).astype(BF16)
    acc = jnp.dot(pm, _with_ones(vm_ref[...]), preferred_element_type=F32)
    for j in range(seq // tk):
        p = jnp.exp2(_nt_dot(q4, k_ref[j * tk:(j + 1) * tk, :]) - bound).astype(BF16)
        acc = acc + jnp.dot(p, _with_ones(v_ref[j * tk:(j + 1) * tk, :]), preferred_element_type=F32)
    _diff_finish([acc[c * tq:(c + 1) * tq] for c in range(4)], sc_ref[0], gh_ref[...], o_ref)


def _diff_online_kernel(sc_ref, q_ref, k_ref, v_ref, km_ref, vm_ref, gh_ref, o_ref, m_ref, acc_ref,
                        *, tq, tk, seq):
    qms = _quarter_masked(q_ref[...])
    km = km_ref[...]
    vm = _with_ones(vm_ref[...])
    for c in range(4):
        s = _nt_dot(qms[c], km)
        m = jnp.max(s, axis=-1, keepdims=True)
        p = jnp.exp2(s - m)
        m_ref[c] = m
        acc_ref[c] = jnp.dot(p.astype(BF16), vm, preferred_element_type=F32)

    def body(j, carry):
        off = pl.multiple_of(j * tk, tk)
        kt = k_ref[pl.ds(off, tk), :]
        vt = _with_ones(v_ref[pl.ds(off, tk), :])
        for c in range(4):
            s = _nt_dot(qms[c], kt)
            m_old = m_ref[c]
            m_new = jnp.maximum(m_old, jnp.max(s, axis=-1, keepdims=True))
            p = jnp.exp2(s - m_new)
            acc_ref[c] = (jnp.exp2(m_old - m_new) * acc_ref[c]
                          + jnp.dot(p.astype(BF16), vt, preferred_element_type=F32))
            m_ref[c] = m_new
        return carry

    lax.fori_loop(0, seq // tk, body, 0)
    _diff_finish([acc_ref[c] for c in range(4)], sc_ref[0], gh_ref[...], o_ref)


def _diff(proj, proj_meta, scalars, gh, tq, tk, bounded):
    b, seq, _ = proj.shape
    n_pairs = DIFF_H // 2
    if bounded:
        kern = functools.partial(_diff_bounded_kernel, tq=tq, tk=tk, seq=seq)
        scratch = []
    else:
        kern = functools.partial(_diff_online_kernel, tq=tq, tk=tk, seq=seq)
        scratch = [pltpu.VMEM((4, tq, 1), F32), pltpu.VMEM((4, tq, 2 * LANES), F32)]
    return pl.pallas_call(
        kern,
        grid=(b, n_pairs, seq // tq),
        in_specs=[
            pl.BlockSpec(memory_space=pltpu.SMEM),
            pl.BlockSpec((None, tq, LANES), lambda bi, hp, i: (bi, i, G_QB + hp)),
            pl.BlockSpec((None, seq, LANES), lambda bi, hp, i: (bi, 0, G_KB + hp)),
            pl.BlockSpec((None, seq, LANES), lambda bi, hp, i: (bi, 0, G_VB + hp)),
            pl.BlockSpec((N_META, LANES), lambda bi, hp, i: (0, G_KB + hp)),
            pl.BlockSpec((N_META, LANES), lambda bi, hp, i: (0, G_VB + hp)),
            pl.BlockSpec((1, LANES), lambda bi, hp, i: (0, 0)),
        ],
        out_specs=pl.BlockSpec((None, tq, LANES), lambda bi, hp, i: (bi, i, hp)),
        out_shape=jax.ShapeDtypeStruct((b, seq, n_pairs * LANES), BF16),
        scratch_shapes=scratch,
        compiler_params=_cparams(("parallel", "parallel", "parallel")),
        name="diff_bounded" if bounded else "diff_online",
    )(scalars, proj, proj, proj, proj_meta, proj_meta, gh)


def _outproj_kernel(ma_ref, mb_ref, x_ref, wo_ref, g_ref, wrh_ref, wrl_ref, br_ref, tri_ref,
                    h1_ref, v_ref, idx_ref, gate_ref, rank_ref, cnt_ref, run_ref, carry_ref, *, tm):
    half = wo_ref.shape[0] // 2

    @pl.when(pl.program_id(0) == 0)
    def _():
        carry_ref[...] = jnp.zeros_like(carry_ref)

    h1 = (x_ref[...]
          + jnp.dot(ma_ref[...], wo_ref[:half, :], preferred_element_type=F32)
          + jnp.dot(mb_ref[...], wo_ref[half:, :], preferred_element_type=F32))
    h1_ref[...] = h1
    v = h1 * lax.rsqrt(jnp.mean(h1 * h1, axis=-1, keepdims=True) + EPS) * g_ref[...]
    v_ref[...] = v
    vh = v.astype(BF16)
    vl = (v - vh.astype(F32)).astype(BF16)
    work = (_nt_dot(wrh_ref[...], vh) + _nt_dot(wrl_ref[...], vh) + _nt_dot(wrh_ref[...], vl)
            + br_ref[...])
    iota_e = lax.broadcasted_iota(jnp.int32, (N_EXPERTS, tm), 0)
    vals, idxs, sels = [], [], []
    for _ in range(TOP_K):
        mk = jnp.max(work, axis=0, keepdims=True)
        ik = jnp.min(jnp.where(work == mk, iota_e, N_EXPERTS), axis=0, keepdims=True)
        sel = iota_e == ik
        work = jnp.where(sel, -jnp.inf, work)
        vals.append(mk)
        idxs.append(ik)
        sels.append(sel)
    exps = [jnp.exp(vk - vals[0]) for vk in vals]
    denom = exps[0] + exps[1] + exps[2] + exps[3]
    gate_ref[...] = jnp.concatenate([e / denom for e in exps], axis=0)
    idx_ref[...] = jnp.concatenate(idxs, axis=0)
    cnt = jnp.zeros((N_EXPERTS, tm), F32)
    for sel in sels:
        cnt = cnt + jnp.where(sel, 1.0, 0.0)
    before = jnp.dot(cnt.astype(BF16), tri_ref[...], preferred_element_type=F32) + carry_ref[...]
    ranks = [jnp.sum(jnp.where(sel, before, 0.0), axis=0, keepdims=True) for sel in sels]
    rank_ref[...] = jnp.concatenate(ranks, axis=0).astype(jnp.int32)
    run = carry_ref[...]
    for j in range(run_ref.shape[0]):
        run_ref[j] = jnp.broadcast_to(run, run_ref.shape[1:])
        run = run + jnp.sum(cnt[:, j * COMBINE_TOKENS:(j + 1) * COMBINE_TOKENS], axis=1, keepdims=True)
    carry_ref[...] = run
    cnt_ref[...] = jnp.broadcast_to(run, cnt_ref.shape)


def _outproj(mixed_a, mixed_b, x2, wo, g, wrh, wrl, br, tri, tm, row0, t):
    d = x2.shape[1]
    hw = mixed_a.shape[1]
    kern = functools.partial(_outproj_kernel, tm=tm)
    blk0 = row0 // tm
    src = lambda i: (i + blk0, 0)
    row = lambda i: (i, 0)
    fix = lambda i: (0, 0)
    col = lambda i: (0, i)
    return pl.pallas_call(
        kern,
        grid=(t // tm,),
        in_specs=[
            pl.BlockSpec((tm, hw), src), pl.BlockSpec((tm, hw), src), pl.BlockSpec((tm, d), src),
            pl.BlockSpec((d, d), fix), pl.BlockSpec((1, d), fix),
            pl.BlockSpec((N_EXPERTS, d), fix), pl.BlockSpec((N_EXPERTS, d), fix),
            pl.BlockSpec((N_EXPERTS, 1), fix), pl.BlockSpec((tm, tm), fix),
        ],
        out_specs=[
            pl.BlockSpec((tm, d), row), pl.BlockSpec((tm, d), row),
            pl.BlockSpec((TOP_K, tm), col), pl.BlockSpec((TOP_K, tm), col),
            pl.BlockSpec((TOP_K, tm), col), pl.BlockSpec((N_EXPERTS, LANES), fix),
            pl.BlockSpec((tm // COMBINE_TOKENS, N_EXPERTS, LANES), lambda i: (i, 0, 0)),
        ],
        out_shape=[
            jax.ShapeDtypeStruct((t, d), F32), jax.ShapeDtypeStruct((t, d), F32),
            jax.ShapeDtypeStruct((TOP_K, t), jnp.int32), jax.ShapeDtypeStruct((TOP_K, t), F32),
            jax.ShapeDtypeStruct((TOP_K, t), jnp.int32),
            jax.ShapeDtypeStruct((N_EXPERTS, LANES), F32),
            jax.ShapeDtypeStruct((t // COMBINE_TOKENS, N_EXPERTS, LANES), F32),
        ],
        scratch_shapes=[pltpu.VMEM((N_EXPERTS, 1), F32)],
        compiler_params=_cparams(("arbitrary",)), name="outproj",
    )(mixed_a, mixed_b, x2, wo, g, wrh, wrl, br, tri)


def _zero_fill(fill_ref, xs_hbm, zero_ref, zsem, tme):
    n_tiles = xs_hbm.shape[0] // tme
    zero_ref[...] = jnp.zeros_like(zero_ref)

    def row_copy(r):
        return pltpu.make_async_copy(zero_ref.at[pl.ds(0, 1)], xs_hbm.at[pl.ds(r, 1)], zsem)

    def chunk_copy(c):
        return pltpu.make_async_copy(zero_ref, xs_hbm.at[pl.ds(c * ZERO_ROWS, ZERO_ROWS)], zsem)

    def fill(start):
        for e in range(N_EXPERTS):
            def one_row(r, carry):
                row_copy(r).start() if start else row_copy(r).wait()
                return carry
            lax.fori_loop(fill_ref[e], fill_ref[N_EXPERTS + e], one_row, 0)

        def one_chunk(c, carry):
            chunk_copy(c).start() if start else chunk_copy(c).wait()
            return carry
        per_tile = tme // ZERO_ROWS
        lax.fori_loop(fill_ref[2 * N_EXPERTS] * per_tile, n_tiles * per_tile, one_chunk, 0)

    fill(True)
    fill(False)


def _issue_rows(dest_ref, v_ref, xs_hbm, sem, t):
    for k in range(TOP_K):
        pltpu.make_async_copy(v_ref.at[pl.ds(t, 1)],
                              xs_hbm.at[pl.ds(dest_ref[TOP_K * t + k], 1)], sem).start(priority=k % 2)


def _wait_rows(v_ref, xs_hbm, sem):
    for _ in range(TOP_K):
        pltpu.make_async_copy(v_ref, xs_hbm.at[pl.ds(0, v_ref.shape[0])], sem).wait()


def _dispatch_kernel(fill_ref, dest_ref, v_ref, xs_hbm, zero_ref, sem, zsem, *, tmd, tme):
    @pl.when(pl.program_id(0) == 0)
    def _():
        _zero_fill(fill_ref, xs_hbm, zero_ref, zsem, tme)

    def issue(t, carry):
        _issue_rows(dest_ref, v_ref, xs_hbm, sem, t)
        return carry

    lax.fori_loop(0, tmd, issue, 0, unroll=8)
    _wait_rows(v_ref, xs_hbm, sem)


def _dispatch(fill_meta, dest_flat, v, n_slots, tmd, tme):
    t, d = v.shape
    kern = functools.partial(_dispatch_kernel, tmd=tmd, tme=tme)
    grid_spec = pltpu.PrefetchScalarGridSpec(
        num_scalar_prefetch=1,
        grid=(t // tmd,),
        in_specs=[
            pl.BlockSpec((TOP_K * tmd,), lambda i, fm: (i,), memory_space=pltpu.SMEM),
            pl.BlockSpec((tmd, d), lambda i, fm: (i, 0)),
        ],
        out_specs=pl.BlockSpec(memory_space=pl.ANY),
        scratch_shapes=[pltpu.VMEM((ZERO_ROWS, d), F32), pltpu.SemaphoreType.DMA, pltpu.SemaphoreType.DMA],
    )
    return pl.pallas_call(
        kern,
        grid_spec=grid_spec,
        out_shape=jax.ShapeDtypeStruct((n_slots, d), F32),
        compiler_params=_cparams(("arbitrary",)), name="dispatch",
    )(fill_meta, dest_flat, v)


def _expert_mlp(xs_ref, w1_ref, b1_ref, w2_ref, b2_ref, ys_ref):
    d_ff = w2_ref.shape[0]
    x = xs_ref[...]
    y = jnp.broadcast_to(b2_ref[...], ys_ref.shape)
    for c0 in range(0, d_ff, FF_CHUNK):
        cg, cl = slice(c0, c0 + FF_CHUNK), slice(d_ff + c0, d_ff + c0 + FF_CHUNK)
        glu = jnp.dot(x, w1_ref[:, cg], preferred_element_type=F32) + b1_ref[:, cg]
        lin = jnp.dot(x, w1_ref[:, cl], preferred_element_type=F32) + b1_ref[:, cl]
        glu = jnp.minimum(glu, SWIGLU_LIMIT)
        lin = jnp.clip(lin, -SWIGLU_LIMIT, SWIGLU_LIMIT)
        act = glu * jax.nn.sigmoid(SWIGLU_ALPHA * glu) * (lin + 1.0)
        y = y + jnp.dot(act, w2_ref[cg, :], preferred_element_type=F32)
    ys_ref[...] = y


def _experts_kernel(te_ref, ts_ref, tv_ref, xs_ref, w1_ref, b1_ref, w2_ref, b2_ref, ys_ref):
    del te_ref, ts_ref

    @pl.when(tv_ref[pl.program_id(0)] > 0)
    def _():
        _expert_mlp(xs_ref, w1_ref, b1_ref, w2_ref, b2_ref, ys_ref)

    @pl.when(tv_ref[pl.program_id(0)] == 0)
    def _():
        ys_ref[...] = jnp.zeros_like(ys_ref)


def _experts(tile_expert, tile_src, tile_valid, xs, w1, b1, w2, b2, tm):
    n_slots, d = xs.shape
    two_ff = w1.shape[2]
    d_ff = two_ff // 2
    grid_spec = pltpu.PrefetchScalarGridSpec(
        num_scalar_prefetch=3,
        grid=(n_slots // tm,),
        in_specs=[
            pl.BlockSpec((tm, d), lambda i, te, ts, tv: (ts[i], 0)),
            pl.BlockSpec((None, d, two_ff), lambda i, te, ts, tv: (te[i], 0, 0)),
            pl.BlockSpec((None, 1, two_ff), lambda i, te, ts, tv: (te[i], 0, 0)),
            pl.BlockSpec((None, d_ff, d), lambda i, te, ts, tv: (te[i], 0, 0)),
            pl.BlockSpec((None, 1, d), lambda i, te, ts, tv: (te[i], 0, 0)),
        ],
        out_specs=pl.BlockSpec((tm, d), lambda i, te, ts, tv: (i, 0)),
    )
    return pl.pallas_call(
        _experts_kernel, grid_spec=grid_spec, out_shape=jax.ShapeDtypeStruct(xs.shape, F32),
        compiler_params=_cparams(("arbitrary",)), name="experts",
    )(tile_expert, tile_src, tile_valid, xs, w1, b1, w2, b2)


def _combine_kernel(src_ref, src_next_ref, pos_ref, gate_ref, h1_ref, ys_hbm, o_ref, buf_ref, sem):
    i = pl.program_id(0)
    n = pl.num_programs(0)
    group_rows = CHUNK_GROUP * RUN_CHUNK

    def issue(s_ref, slot):
        def one_group(g, carry):
            for u in range(CHUNK_GROUP):
                q = g * CHUNK_GROUP + u
                src = pl.multiple_of(s_ref[q], RUN_CHUNK)
                dst = pl.multiple_of(q * RUN_CHUNK, RUN_CHUNK)
                pltpu.make_async_copy(ys_hbm.at[pl.ds(src, RUN_CHUNK)],
                                      buf_ref.at[slot, pl.ds(dst, RUN_CHUNK)],
                                      sem.at[slot]).start(priority=u % 2)
            return carry
        lax.fori_loop(0, s_ref[CHUNK_LIST - 1], one_group, 0)

    @pl.when(i == 0)
    def _():
        buf_ref[...] = jnp.zeros_like(buf_ref)
        issue(src_ref, 0)

    slot = i % 2

    @pl.when(i + 1 < n)
    def _():
        issue(src_next_ref, 1 - slot)

    def wait_group(g, carry):
        pltpu.make_async_copy(buf_ref.at[slot, pl.ds(0, group_rows)],
                              buf_ref.at[slot, pl.ds(0, group_rows)], sem.at[slot]).wait()
        return carry
    lax.fori_loop(0, src_ref[CHUNK_LIST - 1], wait_group, 0)
    rows = buf_ref[slot].astype(BF16)
    col = lax.broadcasted_iota(jnp.int32, (pos_ref.shape[0], COMBINE_ROWS), 1)
    g = jnp.zeros(col.shape, F32)
    for k in reversed(range(TOP_K)):
        g = jnp.where(col == pos_ref[:, k:k + 1], gate_ref[:, k:k + 1], g)
    o_ref[...] = h1_ref[...] + jnp.dot(g.astype(BF16), rows, preferred_element_type=F32)


def _combine(chunk_src, pos_tk, gate_tk, h1, ys):
    t, d = h1.shape
    tmc = COMBINE_TOKENS
    n = t // tmc
    return pl.pallas_call(
        _combine_kernel,
        grid=(n,),
        in_specs=[
            pl.BlockSpec((CHUNK_LIST,), lambda i: (i,), memory_space=pltpu.SMEM),
            pl.BlockSpec((CHUNK_LIST,), lambda i: (jnp.minimum(i + 1, n - 1),), memory_space=pltpu.SMEM),
            pl.BlockSpec((tmc, TOP_K), lambda i: (i, 0)),
            pl.BlockSpec((tmc, TOP_K), lambda i: (i, 0)),
            pl.BlockSpec((tmc, d), lambda i: (i, 0)),
            pl.BlockSpec(memory_space=pl.ANY),
        ],
        out_specs=pl.BlockSpec((tmc, d), lambda i: (i, 0)),
        out_shape=jax.ShapeDtypeStruct((t, d), F32),
        scratch_shapes=[
            pltpu.VMEM((2, COMBINE_ROWS, d), F32),
            pltpu.SemaphoreType.DMA((2,)),
        ],
        compiler_params=_cparams(("arbitrary",)), name="combine",
    )(chunk_src, chunk_src, pos_tk, gate_tk, h1, ys)


def _swap_halves(g, dh):
    return jnp.concatenate([g[dh // 2:], g[:dh // 2]])


def _rope_tables(pos, dh, gq, gk, q_scale):
    inv = ROPE_THETA ** (-jnp.arange(0, dh, 2, dtype=F32) / dh)
    ang = pos.astype(F32)[:, None] * inv[None, :]
    cos = jnp.concatenate([jnp.cos(ang), jnp.cos(ang)], axis=1)
    sin = jnp.concatenate([-jnp.sin(ang), jnp.sin(ang)], axis=1)
    rep = LANES // dh
    tile = lambda a: jnp.tile(a, (1, rep))
    gq, gk = gq.astype(F32), gk.astype(F32)
    return jnp.stack([
        tile(cos * gq[None]) * q_scale, tile(sin * _swap_halves(gq, dh)[None]) * q_scale,
        tile(cos * gk[None]), tile(sin * _swap_halves(gk, dh)[None]),
    ])


def _rope_matrix(dh):
    j = np.arange(LANES)
    swap = (j // dh) * dh + (j % dh + dh // 2) % dh
    m = np.zeros((2 * LANES, 2 * LANES), np.float32)
    m[swap, j] = 1.0
    m[LANES:, LANES:] = (j[:, None] // dh == j[None, :] // dh)
    return jnp.asarray(m, BF16)


def _pick_tile(n, pref):
    t = min(n, pref)
    while n % t:
        t //= 2
    return t


def _tiles(seq, t):
    return dict(
        inproj=_pick_tile(seq, 1024),
        swa_q=256, swa_win=512,
        diff_q=_pick_tile(seq, 512),
        diff_k=_pick_tile(seq, 512),
        outproj=_pick_tile(seq, 1024),
        dispatch=_pick_tile(t, 2048),
    )


def kernel(x, meta_tokens, g_attn, w_in, g_q_swa, g_k_swa, g_q_diff, g_k_diff, sink_swa, lambda_q1, lambda_k1, lambda_q2, lambda_k2, g_diff_head, w_out, g_ffn, w_router, b_router, w_mlp1, b_mlp1, w_mlp2, b_mlp2):
    b, seq, d = x.shape
    assert g_attn.shape[0] == 1 and d % LANES == 0 and seq % 512 == 0
    t = b * seq
    lambda_init = 0.8 - 0.6 * math.exp(-0.3 * 0)

    wi = w_in[0]
    q_a, k_a, v_a, q_b, k_b, v_b = (wi[:, s:e] for s, e in
                                    ((0, 512), (512, 640), (640, 768), (768, 1280), (1280, 1792), (1792, 2304)))
    dup = lambda w: jnp.concatenate([w[:, :64], w[:, :64], w[:, 64:], w[:, 64:]], axis=1)
    w_wide = jnp.concatenate([q_a, dup(k_a), dup(v_a), q_b, k_b, v_b], axis=1).astype(BF16)
    mat_a, mat_b = _rope_matrix(SWA_DH), _rope_matrix(DIFF_DQ)

    def tables(pos):
        return jnp.concatenate([
            _rope_tables(pos, SWA_DH, g_q_swa[0], g_k_swa[0], SWA_DH ** -0.5 * LOG2E),
            _rope_tables(pos, DIFF_DQ, g_q_diff[0], g_k_diff[0], DIFF_DQ ** -0.5 * LOG2E)])

    tab_tok = tables(jnp.arange(N_META, N_META + seq))
    tab_meta = tables(jnp.arange(N_META))
    g_attn2 = g_attn[0].reshape(1, d).astype(F32)

    x2 = x.reshape(t, d)
    tiles = _tiles(seq, t)
    proj = _inproj(x2, g_attn2, w_wide, tab_tok, mat_a, mat_b, tiles["inproj"]).reshape(b, seq, PROJ_W)
    proj_meta = _inproj(meta_tokens.astype(F32), g_attn2, w_wide, tab_meta, mat_a, mat_b, N_META)

    def score_bound(dh, gq, gk):
        return dh ** 0.5 * LOG2E * jnp.max(jnp.abs(gq.astype(F32))) * jnp.max(jnp.abs(gk.astype(F32)))

    bound_a = score_bound(SWA_DH, g_q_swa[0], g_k_swa[0])
    swa_scalars = jnp.concatenate([sink_swa[0].astype(F32) * LOG2E, bound_a[None]])
    mixed_a = lax.cond(
        bound_a <= MAX_SHIFT_BOUND,
        lambda: _swa(proj, proj_meta, swa_scalars, tiles["swa_q"], tiles["swa_win"], True),
        lambda: _swa(proj, proj_meta, swa_scalars, tiles["swa_q"], tiles["swa_win"], False))
    lam = (jnp.exp(jnp.sum(lambda_q1[0].astype(F32) * lambda_k1[0].astype(F32)))
           - jnp.exp(jnp.sum(lambda_q2[0].astype(F32) * lambda_k2[0].astype(F32))) + lambda_init)
    gh = (jnp.tile(g_diff_head[0].astype(F32), 2) * (1.0 - lambda_init)).reshape(1, LANES)
    bound_b = score_bound(DIFF_DQ, g_q_diff[0], g_k_diff[0])
    scalars = jnp.stack([lam, bound_b])
    mixed_b = lax.cond(
        bound_b <= MAX_SHIFT_BOUND,
        lambda: _diff(proj, proj_meta, scalars, gh, tiles["diff_q"], tiles["diff_k"], True),
        lambda: _diff(proj, proj_meta, scalars, gh, tiles["diff_q"], tiles["diff_k"], False))

    wr = w_router[0].T.astype(F32)
    wrh = wr.astype(BF16)
    wrl = (wr - wrh.astype(F32)).astype(BF16)
    tmo = tiles["outproj"]
    tri = jnp.asarray(np.triu(np.ones((tmo, tmo), np.float32), 1), BF16)
    h1, v, idx_t, gate_t, rank_t, cnt, runs = _outproj(
        mixed_a.reshape(t, -1), mixed_b.reshape(t, -1), x2, w_out[0].astype(BF16),
        g_ffn[0].reshape(1, d).astype(F32), wrh, wrl, b_router[0].reshape(N_EXPERTS, 1).astype(F32),
        tri, tmo, 0, t)

    tme = EXPERT_TILE
    experts = jnp.arange(N_EXPERTS, dtype=jnp.int32)

    def per_assignment(table_te):
        tab = jnp.repeat(table_te, COMBINE_TOKENS, axis=0).T
        return jnp.sum(jnp.where(idx_t[None] == experts[:, None, None], tab[:, None, :], 0), axis=0)

    counts = cnt[:, 0].astype(jnp.int32)
    padded = (counts + tme - 1) // tme * tme
    pad_end = jnp.cumsum(padded)
    pad_start = pad_end - padded
    before = runs[:, :, 0].astype(jnp.int32)
    dest_flat = (per_assignment(jnp.broadcast_to(pad_start, before.shape)) + rank_t).T.reshape(-1)
    n_tiles = -(-(TOP_K * t) // tme) + N_EXPERTS
    n_slots = n_tiles * tme
    tile_ids = jnp.arange(n_tiles, dtype=jnp.int32)
    n_valid = pad_end[-1] // tme
    tile_src = jnp.minimum(tile_ids, n_valid - 1)
    tile_expert = jnp.minimum(jnp.sum(pad_end[None, :] <= (tile_src * tme)[:, None], axis=1),
                              N_EXPERTS - 1).astype(jnp.int32)
    tile_valid = (tile_ids < n_valid).astype(jnp.int32)
    in_tile = jnp.concatenate([before[1:], counts[None]]) - before
    run_start = pad_start[None] + before
    lead = run_start % RUN_CHUNK
    n_chunks = jnp.where(in_tile > 0, (lead + in_tile + RUN_CHUNK - 1) // RUN_CHUNK, 0)
    chunk_end = jnp.cumsum(n_chunks, axis=1)
    chunk_first = chunk_end - n_chunks
    pos_tk = (per_assignment(RUN_CHUNK * chunk_first + lead - before) + rank_t).T
    q = jnp.arange(CHUNK_LIST, dtype=jnp.int32)
    owner = jnp.sum(chunk_end[:, None, :] <= q[None, :, None], axis=2)
    is_owner = owner[:, :, None] == experts[None, None, :]
    src_of = (run_start - lead - RUN_CHUNK * chunk_first)[:, None, :] + RUN_CHUNK * q[None, :, None]
    chunk_src = jnp.sum(jnp.where(is_owner, src_of, 0), axis=2)
    n_groups = (chunk_end[:, -1] + CHUNK_GROUP - 1) // CHUNK_GROUP
    chunk_src = chunk_src.at[:, CHUNK_LIST - 1].set(n_groups).reshape(-1)

    fill_meta = jnp.concatenate([pad_start + counts, pad_end, n_valid[None]]).astype(jnp.int32)
    xs = _dispatch(fill_meta, dest_flat, v, n_slots, tiles["dispatch"], tme)
    ys = _experts(tile_expert, tile_src, tile_valid, xs,
                  w_mlp1[0], b_mlp1[0].reshape(N_EXPERTS, 1, -1).astype(F32),
                  w_mlp2[0], b_mlp2[0].reshape(N_EXPERTS, 1, -1).astype(F32), tme)
    out = _combine(chunk_src, pos_tk, gate_t.T, h1, ys)
    return out.reshape(b, seq, d)
```
